```python
import math
import jax, jax.numpy as jnp
from jax import lax
import numpy as np

D_MODEL = 4096
BATCH = 4
SEQ = 4096
DEPTH = 1

HEAD_DIM = 128
SB_HEADS = 16
FX_HEADS = 16
SB_WIDTH = SB_HEADS * HEAD_DIM
FX_WIDTH = FX_HEADS * HEAD_DIM
N_BRANCHES = 2
Q_BLOCK = 128
SPLIT_SIZES = [SB_WIDTH] * 3 + [FX_WIDTH] * 3 + [FX_HEADS, N_BRANCHES * D_MODEL]
SPLIT_OFFSETS = [int(o) for o in np.cumsum(SPLIT_SIZES)[:-1]]
PROJ_WIDTH = int(sum(SPLIT_SIZES))
N_EXPERTS = 32
TOP_K = 4
D_FF = D_MODEL // 2
SWIGLU_ALPHA = 1.702
SWIGLU_LIMIT = 7.0
EXPERT_BLOCK = 512
LN_EPS = 1e-5
DEEPNORM_ALPHA = (2.0 * DEPTH) ** 0.25
DEEPNORM_BETA = (8.0 * DEPTH) ** -0.25

kernel_name = "stickbreak_fox_gated_moe_deepnorm"


def layer_norm(x, g, b):
    xf = x.astype(jnp.float32)
    mu = jnp.mean(xf, axis=-1, keepdims=True)
    var = jnp.mean(jnp.square(xf - mu), axis=-1, keepdims=True)
    y = (xf - mu) * lax.rsqrt(var + LN_EPS)
    return (y * g.astype(jnp.float32) + b.astype(jnp.float32)).astype(x.dtype)


def to_heads(t, n_heads):
    b, s, _ = t.shape
    return t.reshape(b, s, n_heads, HEAD_DIM).transpose(0, 2, 1, 3)


def to_query_blocks(t):
    b, h, s = t.shape[:3]
    nb = s // Q_BLOCK
    t = t.reshape((b, h, nb, Q_BLOCK) + t.shape[3:])
    return jnp.moveaxis(t, 2, 0)


def from_query_blocks(o):
    nb, b, h, qb, d = o.shape
    return o.transpose(1, 0, 3, 2, 4).reshape(b, nb * qb, h * d)


def stick_breaking_attention(q, k, v):
    s_len = q.shape[2]
    scale = HEAD_DIM ** -0.5
    kpos = jnp.arange(s_len)
    starts = jnp.arange(s_len // Q_BLOCK) * Q_BLOCK

    def block(args):
        qb, start = args
        qpos = start + jnp.arange(Q_BLOCK)
        strict = kpos[None, :] < qpos[:, None]
        z = jnp.einsum('bhqd,bhkd->bhqk', qb, k).astype(jnp.float32) * scale
        log_beta = jax.nn.log_sigmoid(z)
        log_keep = jnp.where(strict, jax.nn.log_sigmoid(-z), 0.0)
        after = lax.cumsum(log_keep, axis=3, reverse=True) - log_keep
        w = jnp.where(strict, jnp.exp(log_beta + after), 0.0)
        return jnp.einsum('bhqk,bhkd->bhqd', w.astype(v.dtype), v)

    return from_query_blocks(lax.map(block, (to_query_blocks(q), starts)))


def forgetting_attention(q, k, v, log_f):
    s_len = q.shape[2]
    scale = HEAD_DIM ** -0.5
    kpos = jnp.arange(s_len)
    starts = jnp.arange(s_len // Q_BLOCK) * Q_BLOCK
    c = jnp.cumsum(log_f, axis=-1)

    def block(args):
        qb, cq, start = args
        qpos = start + jnp.arange(Q_BLOCK)
        causal = kpos[None, :] <= qpos[:, None]
        z = jnp.einsum('bhqd,bhkd->bhqk', qb, k).astype(jnp.float32) * scale
        logits = z + cq[..., None] - c[:, :, None, :]
        p = jax.nn.softmax(jnp.where(causal, logits, -jnp.inf), axis=-1)
        return jnp.einsum('bhqk,bhkd->bhqd', p.astype(v.dtype), v)

    return from_query_blocks(lax.map(block, (to_query_blocks(q), to_query_blocks(c), starts)))


def token_mixer(x, w_in, b_in, w_branch_sb, w_branch_fx, w_out):
    proj = x @ w_in + b_in
    q_sb, k_sb, v_sb, q_fx, k_fx, v_fx, f_logit, gates = jnp.split(proj, SPLIT_OFFSETS, axis=-1)
    o_sb = stick_breaking_attention(to_heads(q_sb, SB_HEADS), to_heads(k_sb, SB_HEADS),
                                    to_heads(v_sb, SB_HEADS))
    log_f = jax.nn.log_sigmoid(f_logit.astype(jnp.float32)).transpose(0, 2, 1)
    o_fx = forgetting_attention(to_heads(q_fx, FX_HEADS), to_heads(k_fx, FX_HEADS),
                                to_heads(v_fx, FX_HEADS), log_f)
    g_sb, g_fx = jnp.split(jax.nn.sigmoid(gates), N_BRANCHES, axis=-1)
    merged = g_sb * (o_sb @ w_branch_sb) + g_fx * (o_fx @ w_branch_fx)
    return merged @ w_out


def moe(h, w_router, b_router, w_up, b_up, w_down, b_down):
    b, s, d = h.shape
    n_tok = b * s
    n_assign = n_tok * TOP_K
    tok = h.reshape(n_tok, d)
    logits = (tok @ w_router + b_router).astype(jnp.float32)
    top_vals, top_idx = lax.top_k(logits, TOP_K)
    gate = jax.nn.softmax(top_vals, axis=-1)

    flat_e = top_idx.reshape(-1)
    flat_tok = jnp.repeat(jnp.arange(n_tok), TOP_K)
    flat_gate = gate.reshape(-1)
    order = jnp.argsort(flat_e)
    sorted_e = flat_e[order]
    sorted_tok = flat_tok[order]
    sorted_gate = flat_gate[order]

    counts = jnp.bincount(flat_e, length=N_EXPERTS)
    padded = (counts + EXPERT_BLOCK - 1) // EXPERT_BLOCK * EXPERT_BLOCK
    start = jnp.cumsum(counts) - counts
    pend = jnp.cumsum(padded)
    pstart = pend - padded
    dest = pstart[sorted_e] + (jnp.arange(n_assign) - start[sorted_e])

    n_blk = -(-n_assign // EXPERT_BLOCK) + N_EXPERTS
    n_rows = n_blk * EXPERT_BLOCK
    rows = jnp.zeros((n_rows, d), tok.dtype).at[dest].set(tok[sorted_tok])
    blk_start = jnp.arange(n_blk) * EXPERT_BLOCK
    blk_e = jnp.clip(jnp.searchsorted(pend, blk_start, side='right'), 0, N_EXPERTS - 1)

    def expert_block(args):
        xb, e = args
        hu = xb @ w_up[e] + b_up[e]
        g, u = hu[:, :D_FF], hu[:, D_FF:]
        g = jnp.minimum(g, SWIGLU_LIMIT)
        u = jnp.clip(u, -SWIGLU_LIMIT, SWIGLU_LIMIT)
        act = g * jax.nn.sigmoid(SWIGLU_ALPHA * g) * (u + 1.0)
        return act @ w_down[e] + b_down[e]

    y_rows = lax.map(expert_block, (rows.reshape(n_blk, EXPERT_BLOCK, d), blk_e)).reshape(n_rows, d)
    y_assign = y_rows[dest] * sorted_gate[:, None].astype(y_rows.dtype)
    out = jax.ops.segment_sum(y_assign, sorted_tok, num_segments=n_tok)
    return out.reshape(b, s, d)


def setup_inputs(seed: int = 0) -> dict:
    key = jax.random.key(seed)
    ks = jax.random.split(key, 20)
    f32 = jnp.float32
    nrm = lambda k, shape, scale: jax.random.normal(k, shape, f32) * scale
    x = jax.random.normal(ks[0], (BATCH, SEQ, D_MODEL), f32)
    w_in = nrm(ks[1], (DEPTH, D_MODEL, PROJ_WIDTH), D_MODEL ** -0.5)
    b_in = jnp.concatenate([
        nrm(ks[2], (DEPTH, 3 * SB_WIDTH + 3 * FX_WIDTH), 0.01),
        jax.random.uniform(ks[3], (DEPTH, FX_HEADS), f32, minval=1.0, maxval=5.0),
        nrm(ks[4], (DEPTH, N_BRANCHES * D_MODEL), 0.01),
    ], axis=-1)
    w_branch_sb = nrm(ks[5], (DEPTH, SB_WIDTH, D_MODEL), SB_WIDTH ** -0.5)
    w_branch_fx = nrm(ks[6], (DEPTH, FX_WIDTH, D_MODEL), FX_WIDTH ** -0.5)
    w_out = nrm(ks[7], (DEPTH, D_MODEL, D_MODEL), DEEPNORM_BETA * D_MODEL ** -0.5)
    ln1_g = 1.0 + nrm(ks[8], (DEPTH, D_MODEL), 0.02)
    ln1_b = nrm(ks[9], (DEPTH, D_MODEL), 0.01)
    w_router = nrm(ks[10], (DEPTH, D_MODEL, N_EXPERTS), D_MODEL ** -0.5)
    b_router = nrm(ks[11], (DEPTH, N_EXPERTS), 0.01)
    w_up = nrm(ks[12], (DEPTH, N_EXPERTS, D_MODEL, 2 * D_FF), D_MODEL ** -0.5)
    b_up = nrm(ks[13], (DEPTH, N_EXPERTS, 2 * D_FF), 0.01)
    w_down = nrm(ks[14], (DEPTH, N_EXPERTS, D_FF, D_MODEL), DEEPNORM_BETA * D_FF ** -0.5)
    b_down = nrm(ks[15], (DEPTH, N_EXPERTS, D_MODEL), 0.01)
    ln2_g = 1.0 + nrm(ks[16], (DEPTH, D_MODEL), 0.02)
    ln2_b = nrm(ks[17], (DEPTH, D_MODEL), 0.01)
    return {"x": x, "w_in": w_in, "b_in": b_in, "w_branch_sb": w_branch_sb,
            "w_branch_fx": w_branch_fx, "w_out": w_out, "ln1_g": ln1_g, "ln1_b": ln1_b,
            "w_router": w_router, "b_router": b_router, "w_up": w_up, "b_up": b_up,
            "w_down": w_down, "b_down": b_down, "ln2_g": ln2_g, "ln2_b": ln2_b}


def reference(x, w_in, b_in, w_branch_sb, w_branch_fx, w_out, ln1_g, ln1_b,
              w_router, b_router, w_up, b_up, w_down, b_down, ln2_g, ln2_b):
    h = x
    for l in range(DEPTH):
        mix = token_mixer(h, w_in[l], b_in[l], w_branch_sb[l], w_branch_fx[l], w_out[l])
        h = layer_norm(DEEPNORM_ALPHA * h + mix, ln1_g[l], ln1_b[l])
        ffn = moe(h, w_router[l], b_router[l], w_up[l], b_up[l], w_down[l], b_down[l])
        h = layer_norm(DEEPNORM_ALPHA * h + ffn, ln2_g[l], ln2_b[l])
    return h
```

```python
import functools
from typing import NamedTuple

import jax
import jax.numpy as jnp
from jax import lax
from jax.experimental import pallas as pl
from jax.experimental.pallas import tpu as pltpu

F32 = jnp.float32
BF16 = jnp.bfloat16

LANES = 128
HEAD_DIM = 128
TOP_K = 4
SWIGLU_ALPHA = 1.702
SWIGLU_LIMIT = 7.0
LN_EPS = 1e-5
MASK_VALUE = -1e30
VMEM_LIMIT_BYTES = 56 * 1024 * 1024


class Config(NamedTuple):
    heads: int
    n_experts: int
    depth: int
    mm_tm: int
    mm_tn: int
    attn_tile: int
    post_tm: int
    post_tc: int
    route_tm: int
    moe_tm: int
    moe_tf: int
    moe_tn: int
    comb_tt: int


def _params(sem):
    return pltpu.CompilerParams(dimension_semantics=sem, vmem_limit_bytes=VMEM_LIMIT_BYTES)


def _matmul_kernel(x_ref, w_ref, b_ref, o_ref, *, act):
    acc = jnp.dot(x_ref[...], w_ref[...], preferred_element_type=F32) + b_ref[...]
    if act == "sigmoid":
        acc = jax.nn.sigmoid(acc)
    o_ref[...] = acc.astype(o_ref.dtype)


def _matmul(x, w, b, *, out_dtype, act, tm, tn, name):
    m, k = x.shape
    n = w.shape[1]
    tm, tn = min(tm, m), min(tn, n)
    return pl.pallas_call(
        functools.partial(_matmul_kernel, act=act),
        out_shape=jax.ShapeDtypeStruct((m, n), out_dtype),
        grid=(m // tm, n // tn),
        in_specs=[pl.BlockSpec((tm, k), lambda i, j: (i, 0)),
                  pl.BlockSpec((k, tn), lambda i, j: (0, j)),
                  pl.BlockSpec((1, tn), lambda i, j: (0, j))],
        out_specs=pl.BlockSpec((tm, tn), lambda i, j: (i, j)),
        compiler_params=_params(("parallel", "arbitrary")),
        name=name,
    )(x, w, b)


def _forget_cumsum_kernel(f_ref, c_ref):
    f = f_ref[...]
    x = jnp.minimum(f, 0.0) - jnp.log(1.0 + jnp.exp(-jnp.abs(f)))
    n = x.shape[1]
    lane = lax.broadcasted_iota(jnp.int32, x.shape, 1)
    d = 1
    while d < n:
        x = x + jnp.where(lane >= d, pltpu.roll(x, d, axis=1), 0.0)
        d *= 2
    c_ref[...] = x


def _forget_cumsum(f_logit_rows):
    return pl.pallas_call(
        _forget_cumsum_kernel,
        out_shape=jax.ShapeDtypeStruct(f_logit_rows.shape, F32),
        name="forget_cumsum",
    )(f_logit_rows)


def _qk(q, k):
    return lax.dot_general(q, k, (((1,), (1,)), ((), ())), preferred_element_type=F32)


def _lane_repeat(x, width):
    return jnp.concatenate([x] * (width // LANES), axis=1)


def _sb_attn_kernel(q_ref, k_ref, v_ref, o_ref, u_ref, acc_ref, carry_ref, *, tile, scale):
    i = pl.program_id(2)
    t = tile
    q = q_ref[0]
    row = lax.broadcasted_iota(jnp.int32, (t, t), 0)
    col = lax.broadcasted_iota(jnp.int32, (t, t), 1)
    strict = col < row
    u_ref[:, :t] = (row > col).astype(BF16)
    u_ref[:, t:] = jnp.ones((t, LANES), BF16)
    acc_ref[...] = jnp.zeros_like(acc_ref)
    carry_ref[...] = jnp.zeros_like(carry_ref)

    def step(j, masked):
        start = pl.multiple_of(j * t, t)
        kj = k_ref[0, pl.ds(start, t), :]
        vj = v_ref[0, pl.ds(start, t), :]
        z = _qk(q, kj) * scale
        log_beta = jnp.minimum(z, 0.0) - jnp.log(1.0 + jnp.exp(-jnp.abs(z)))
        log_keep = log_beta - z
        if masked:
            log_keep = jnp.where(strict, log_keep, 0.0)
        hi = log_keep.astype(BF16)
        lo = (log_keep - hi.astype(F32)).astype(BF16)
        u = u_ref[...]
        sums = (jnp.dot(hi, u, preferred_element_type=F32)
                + jnp.dot(lo, u, preferred_element_type=F32))
        carry = carry_ref[...]
        w = jnp.exp(log_beta + sums[:, :t] + _lane_repeat(carry, t))
        if masked:
            w = jnp.where(strict, w, 0.0)
        acc_ref[...] += jnp.dot(w.astype(BF16), vj, preferred_element_type=F32)
        carry_ref[...] = carry + sums[:, t:]

    step(i, True)

    def body(it, c):
        step(i - 1 - it, False)
        return c

    lax.fori_loop(0, i, body, 0)
    o_ref[0] = acc_ref[...].astype(o_ref.dtype)


def _fx_attn_kernel(q_ref, k_ref, v_ref, c_ref, o_ref, m_ref, acc_ref, *, tile, scale):
    i = pl.program_id(2)
    t = tile
    q = q_ref[0]
    row = lax.broadcasted_iota(jnp.int32, (t, t), 0)
    col = lax.broadcasted_iota(jnp.int32, (t, t), 1)
    causal = col <= row
    m_ref[...] = jnp.full_like(m_ref, MASK_VALUE)
    acc_ref[...] = jnp.zeros_like(acc_ref)
    ones = jnp.ones((t, LANES), BF16)

    def step(j, masked):
        start = pl.multiple_of(j * t, t)
        kj = k_ref[0, pl.ds(start, t), :]
        vj = v_ref[0, pl.ds(start, t), :]
        ck = c_ref[0, 0, :, pl.ds(start, t)]
        s = _qk(q, kj) * scale - ck
        if masked:
            s = jnp.where(causal, s, MASK_VALUE)
        m_prev = m_ref[...]
        m_next = jnp.maximum(m_prev, jnp.max(s, axis=1, keepdims=True))
        p = jnp.exp(s - _lane_repeat(m_next, t))
        alpha = jnp.exp(m_prev - m_next)
        v_ext = jnp.concatenate([vj, ones], axis=1)
        pv = jnp.dot(p.astype(BF16), v_ext, preferred_element_type=F32)
        acc_ref[...] = _lane_repeat(alpha, 2 * LANES) * acc_ref[...] + pv
        m_ref[...] = m_next

    def body(j, c):
        step(j, False)
        return c

    lax.fori_loop(0, i, body, 0)
    step(i, True)
    acc = acc_ref[...]
    o_ref[0] = (acc[:, :LANES] / acc[:, LANES:]).astype(o_ref.dtype)


def _attention(qkv, c_rows, cfg, which):
    b, s, _ = qkv.shape
    h = cfg.heads
    t = min(cfg.attn_tile, s)
    base = 0 if which == "sb" else 3 * h
    scale = HEAD_DIM ** -0.5
    q_spec = pl.BlockSpec((1, t, HEAD_DIM), lambda bi, hi, qi: (bi, qi, base + hi))
    k_spec = pl.BlockSpec((1, s, HEAD_DIM), lambda bi, hi, qi: (bi, 0, base + h + hi))
    v_spec = pl.BlockSpec((1, s, HEAD_DIM), lambda bi, hi, qi: (bi, 0, base + 2 * h + hi))
    o_spec = pl.BlockSpec((1, t, HEAD_DIM), lambda bi, hi, qi: (bi, qi, hi))
    out_shape = jax.ShapeDtypeStruct((b, s, h * HEAD_DIM), BF16)
    grid = (b, h, s // t)
    sem = ("parallel", "parallel", "arbitrary")
    if which == "sb":
        return pl.pallas_call(
            functools.partial(_sb_attn_kernel, tile=t, scale=scale),
            out_shape=out_shape, grid=grid,
            in_specs=[q_spec, k_spec, v_spec], out_specs=o_spec,
            scratch_shapes=[pltpu.VMEM((t, t + LANES), BF16),
                            pltpu.VMEM((t, HEAD_DIM), F32),
                            pltpu.VMEM((t, LANES), F32)],
            compiler_params=_params(sem), name="sb_attention",
        )(qkv, qkv, qkv)
    c_spec = pl.BlockSpec((1, 1, 1, s), lambda bi, hi, qi: (bi, hi, 0, 0))
    return pl.pallas_call(
        functools.partial(_fx_attn_kernel, tile=t, scale=scale),
        out_shape=out_shape, grid=grid,
        in_specs=[q_spec, k_spec, v_spec, c_spec], out_specs=o_spec,
        scratch_shapes=[pltpu.VMEM((t, LANES), F32),
                        pltpu.VMEM((t, 2 * LANES), F32)],
        compiler_params=_params(sem), name="fx_attention",
    )(qkv, qkv, qkv, c_rows)


def _layer_norm(y, g, b):
    mu = jnp.mean(y, axis=-1, keepdims=True)
    d = y - mu
    var = jnp.mean(d * d, axis=-1, keepdims=True)
    return d * lax.rsqrt(var + LN_EPS) * g + b


def _post_attn_kernel(osb_ref, ofx_ref, gsb_ref, gfx_ref, wsb_ref, wfx_ref, wout_ref,
                      x_ref, g_ref, b_ref, h_ref, acc_ref, *, alpha):
    c = pl.program_id(1)

    @pl.when(c == 0)
    def _():
        acc_ref[...] = jnp.zeros_like(acc_ref)

    tsb = jnp.dot(osb_ref[...], wsb_ref[...], preferred_element_type=F32)
    tfx = jnp.dot(ofx_ref[...], wfx_ref[...], preferred_element_type=F32)
    merged = gsb_ref[...].astype(F32) * tsb + gfx_ref[...].astype(F32) * tfx
    acc_ref[...] += jnp.dot(merged.astype(BF16), wout_ref[...], preferred_element_type=F32)

    @pl.when(c == pl.num_programs(1) - 1)
    def _():
        h_ref[...] = _layer_norm(alpha * x_ref[...] + acc_ref[...], g_ref[...], b_ref[...])


def _post_attention(o_sb, o_fx, gates, w_sb, w_fx, w_out, x, ln_g, ln_b, cfg, alpha):
    m, d = x.shape
    width = o_sb.shape[1]
    tm, tc = min(cfg.post_tm, m), min(cfg.post_tc, d)
    nc = d // tc
    return pl.pallas_call(
        functools.partial(_post_attn_kernel, alpha=alpha),
        out_shape=jax.ShapeDtypeStruct((m, d), F32),
        grid=(m // tm, nc),
        in_specs=[pl.BlockSpec((tm, width), lambda i, c: (i, 0)),
                  pl.BlockSpec((tm, width), lambda i, c: (i, 0)),
                  pl.BlockSpec((tm, tc), lambda i, c: (i, c)),
                  pl.BlockSpec((tm, tc), lambda i, c: (i, nc + c)),
                  pl.BlockSpec((width, tc), lambda i, c: (0, c)),
                  pl.BlockSpec((width, tc), lambda i, c: (0, c)),
                  pl.BlockSpec((tc, d), lambda i, c: (c, 0)),
                  pl.BlockSpec((tm, d), lambda i, c: (i, 0)),
                  pl.BlockSpec((1, d), lambda i, c: (0, 0)),
                  pl.BlockSpec((1, d), lambda i, c: (0, 0))],
        out_specs=pl.BlockSpec((tm, d), lambda i, c: (i, 0)),
        scratch_shapes=[pltpu.VMEM((tm, d), F32)],
        compiler_params=_params(("parallel", "arbitrary")),
        name="post_attention",
    )(o_sb, o_fx, gates, gates, w_sb, w_fx, w_out, x, ln_g, ln_b)


def _router_kernel(h_ref, w_ref, b_ref, idx_ref, gate_ref):
    logits = jnp.dot(h_ref[...], w_ref[...], preferred_element_type=F32,
                     precision=lax.Precision.HIGHEST) + b_ref[...]
    lane = lax.broadcasted_iota(jnp.int32, logits.shape, 1)
    lane_f = lane.astype(F32)
    vals = logits
    idx_out = jnp.zeros(logits.shape, F32)
    exp_out = jnp.zeros(logits.shape, F32)
    denom = jnp.zeros((logits.shape[0], 1), F32)
    top = None
    for k in range(TOP_K):
        m = jnp.max(vals, axis=1, keepdims=True)
        sel = jnp.min(jnp.where(vals == m, lane_f, float(LANES)), axis=1, keepdims=True)
        if k == 0:
            top = m
        e = jnp.exp(m - top)
        denom = denom + e
        idx_out = jnp.where(lane == k, sel, idx_out)
        exp_out = jnp.where(lane == k, e, exp_out)
        vals = jnp.where(lane_f == sel, MASK_VALUE * 2.0, vals)
    idx_ref[...] = idx_out.astype(jnp.int32)
    gate_ref[...] = exp_out / denom


def _router(h, w_router, b_router, cfg):
    m, d = h.shape
    e = cfg.n_experts
    tm = min(cfg.route_tm, m)
    w_pad = jnp.zeros((d, LANES), F32).at[:, :e].set(w_router)
    b_pad = jnp.full((1, LANES), MASK_VALUE, F32).at[0, :e].set(b_router)
    idx, gate = pl.pallas_call(
        _router_kernel,
        out_shape=(jax.ShapeDtypeStruct((m, LANES), jnp.int32),
                   jax.ShapeDtypeStruct((m, LANES), F32)),
        grid=(m // tm,),
        in_specs=[pl.BlockSpec((tm, d), lambda i: (i, 0)),
                  pl.BlockSpec((d, LANES), lambda i: (0, 0)),
                  pl.BlockSpec((1, LANES), lambda i: (0, 0))],
        out_specs=(pl.BlockSpec((tm, LANES), lambda i: (i, 0)),
                   pl.BlockSpec((tm, LANES), lambda i: (i, 0))),
        compiler_params=_params(("parallel",)),
        name="router",
    )(h, w_pad, b_pad)
    return idx[:, :TOP_K], gate[:, :TOP_K]


def _row_copy(src_hbm, row, dst_vmem, slot, sem):
    return pltpu.make_async_copy(src_hbm.at[pl.ds(row, 1), :], dst_vmem.at[pl.ds(slot, 1), :], sem)


def _gather_rows_kernel(tok_ref, nused_ref, h_hbm, o_ref, buf_ref, sem, *, tm):
    blk = pl.program_id(0)
    base = blk * tm

    @pl.when(blk < nused_ref[0])
    def _():
        def start(r, c):
            _row_copy(h_hbm, tok_ref[base + r], buf_ref, r, sem).start()
            return c

        def wait(r, c):
            _row_copy(h_hbm, tok_ref[base + r], buf_ref, r, sem).wait()
            return c

        lax.fori_loop(0, tm, start, 0)
        lax.fori_loop(0, tm, wait, 0)
        o_ref[...] = buf_ref[...].astype(o_ref.dtype)

    @pl.when(blk >= nused_ref[0])
    def _():
        o_ref[...] = jnp.zeros_like(o_ref)


def _gather_rows(h, row_tok, n_used, n_blk, cfg):
    d = h.shape[1]
    tm = cfg.moe_tm
    return pl.pallas_call(
        functools.partial(_gather_rows_kernel, tm=tm),
        out_shape=jax.ShapeDtypeStruct((n_blk * tm, d), BF16),
        grid_spec=pltpu.PrefetchScalarGridSpec(
            num_scalar_prefetch=2, grid=(n_blk,),
            in_specs=[pl.BlockSpec(memory_space=pl.ANY)],
            out_specs=pl.BlockSpec((tm, d), lambda i, tok, nu: (i, 0)),
            scratch_shapes=[pltpu.VMEM((tm, d), F32), pltpu.SemaphoreType.DMA]),
        compiler_params=_params(("arbitrary",)),
        name="moe_gather",
    )(row_tok, n_used, h)


def _expert_up_kernel(blk_e_ref, nused_ref, x_ref, wg_ref, wu_ref, bg_ref, bu_ref, act_ref,
                      wg_bf, wu_bf):
    blk = pl.program_id(1)
    e = blk_e_ref[blk]
    e_prev = blk_e_ref[jnp.maximum(blk - 1, 0)]

    @pl.when((blk == 0) | (e != e_prev))
    def _():
        wg_bf[...] = wg_ref[0].astype(BF16)
        wu_bf[...] = wu_ref[0].astype(BF16)

    @pl.when(blk < nused_ref[0])
    def _():
        x = x_ref[...]
        g = jnp.dot(x, wg_bf[...], preferred_element_type=F32) + bg_ref[0]
        u = jnp.dot(x, wu_bf[...], preferred_element_type=F32) + bu_ref[0]
        g = jnp.minimum(g, SWIGLU_LIMIT)
        u = jnp.clip(u, -SWIGLU_LIMIT, SWIGLU_LIMIT)
        act_ref[...] = (g * jax.nn.sigmoid(SWIGLU_ALPHA * g) * (u + 1.0)).astype(act_ref.dtype)

    @pl.when(blk >= nused_ref[0])
    def _():
        act_ref[...] = jnp.zeros_like(act_ref)


def _expert_up(rows, w_up, b_up, blk_e, n_used, cfg):
    n_rows, d = rows.shape
    d_ff = w_up.shape[2] // 2
    tm, tf = cfg.moe_tm, min(cfg.moe_tf, d_ff)
    nf = d_ff // tf
    n_blk = n_rows // tm
    b_up3 = b_up.reshape(b_up.shape[0], 1, 2 * d_ff)
    return pl.pallas_call(
        _expert_up_kernel,
        out_shape=jax.ShapeDtypeStruct((n_rows, d_ff), BF16),
        grid_spec=pltpu.PrefetchScalarGridSpec(
            num_scalar_prefetch=2, grid=(nf, n_blk),
            in_specs=[pl.BlockSpec((tm, d), lambda f, i, be, nu: (i, 0)),
                      pl.BlockSpec((1, d, tf), lambda f, i, be, nu: (be[i], 0, f)),
                      pl.BlockSpec((1, d, tf), lambda f, i, be, nu: (be[i], 0, nf + f)),
                      pl.BlockSpec((1, 1, tf), lambda f, i, be, nu: (be[i], 0, f)),
                      pl.BlockSpec((1, 1, tf), lambda f, i, be, nu: (be[i], 0, nf + f))],
            out_specs=pl.BlockSpec((tm, tf), lambda f, i, be, nu: (i, f)),
            scratch_shapes=[pltpu.VMEM((d, tf), BF16), pltpu.VMEM((d, tf), BF16)]),
        compiler_params=_params(("arbitrary", "arbitrary")),
        name="moe_up",
    )(blk_e, n_used, rows, w_up, w_up, b_up3, b_up3)


def _expert_down_kernel(blk_e_ref, nused_ref, a_ref, w_ref, b_ref, gate_ref, y_ref, w_bf):
    blk = pl.program_id(1)
    e = blk_e_ref[blk]
    e_prev = blk_e_ref[jnp.maximum(blk - 1, 0)]

    @pl.when((blk == 0) | (e != e_prev))
    def _():
        w_bf[...] = w_ref[0].astype(BF16)

    @pl.when(blk < nused_ref[0])
    def _():
        y = jnp.dot(a_ref[...], w_bf[...], preferred_element_type=F32) + b_ref[0]
        y_ref[...] = y * gate_ref[...]

    @pl.when(blk >= nused_ref[0])
    def _():
        y_ref[...] = jnp.zeros_like(y_ref)


def _expert_down(act, w_down, b_down, row_gate, blk_e, n_used, cfg):
    n_rows, d_ff = act.shape
    d = w_down.shape[2]
    tm, tn = cfg.moe_tm, min(cfg.moe_tn, d)
    n_blk = n_rows // tm
    b_down3 = b_down.reshape(b_down.shape[0], 1, d)
    return pl.pallas_call(
        _expert_down_kernel,
        out_shape=jax.ShapeDtypeStruct((n_rows, d), F32),
        grid_spec=pltpu.PrefetchScalarGridSpec(
            num_scalar_prefetch=2, grid=(d // tn, n_blk),
            in_specs=[pl.BlockSpec((tm, d_ff), lambda n, i, be, nu: (i, 0)),
                      pl.BlockSpec((1, d_ff, tn), lambda n, i, be, nu: (be[i], 0, n)),
                      pl.BlockSpec((1, 1, tn), lambda n, i, be, nu: (be[i], 0, n)),
                      pl.BlockSpec((tm, 1), lambda n, i, be, nu: (i, 0))],
            out_specs=pl.BlockSpec((tm, tn), lambda n, i, be, nu: (i, n)),
            scratch_shapes=[pltpu.VMEM((d_ff, tn), BF16)]),
        compiler_params=_params(("arbitrary", "arbitrary")),
        name="moe_down",
    )(blk_e, n_used, act, w_down, b_down3, row_gate)


def _combine_kernel(dest_ref, y_hbm, h_ref, g_ref, b_ref, o_ref, buf_ref, sem, *, tt, alpha):
    base = pl.program_id(0) * (tt * TOP_K)

    def copy(a):
        return _row_copy(y_hbm, dest_ref[base + a], buf_ref, a, sem)

    def start(a, c):
        copy(a).start()
        return c

    def wait(a, c):
        copy(a).wait()
        return c

    lax.fori_loop(0, tt * TOP_K, start, 0)
    lax.fori_loop(0, tt * TOP_K, wait, 0)
    ffn = buf_ref[pl.ds(0, tt), :]
    for k in range(1, TOP_K):
        ffn = ffn + buf_ref[pl.ds(k * tt, tt), :]
    o_ref[...] = _layer_norm(alpha * h_ref[...] + ffn, g_ref[...], b_ref[...])


def _combine(y_rows, dest_kmajor, h, ln_g, ln_b, cfg, alpha):
    m, d = h.shape
    tt = min(cfg.comb_tt, m)
    return pl.pallas_call(
        functools.partial(_combine_kernel, tt=tt, alpha=alpha),
        out_shape=jax.ShapeDtypeStruct((m, d), F32),
        grid_spec=pltpu.PrefetchScalarGridSpec(
            num_scalar_prefetch=1, grid=(m // tt,),
            in_specs=[pl.BlockSpec(memory_space=pl.ANY),
                      pl.BlockSpec((tt, d), lambda i, dst: (i, 0)),
                      pl.BlockSpec((1, d), lambda i, dst: (0, 0)),
                      pl.BlockSpec((1, d), lambda i, dst: (0, 0))],
            out_specs=pl.BlockSpec((tt, d), lambda i, dst: (i, 0)),
            scratch_shapes=[pltpu.VMEM((tt * TOP_K, d), F32), pltpu.SemaphoreType.DMA]),
        compiler_params=_params(("arbitrary",)),
        name="moe_combine",
    )(dest_kmajor, y_rows, h, ln_g, ln_b)


def _routing_tables(top_idx, top_gate, cfg):
    n_tok = top_idx.shape[0]
    e, tm = cfg.n_experts, cfg.moe_tm
    n_assign = n_tok * TOP_K
    n_blk = -(-n_assign // tm) + e
    flat_e = top_idx.reshape(-1)
    flat_gate = top_gate.reshape(-1)
    onehot = (flat_e[:, None] == jnp.arange(e, dtype=jnp.int32)[None, :]).astype(jnp.int32)
    csum = jnp.cumsum(onehot, axis=0)
    rank = jnp.take_along_axis(csum, flat_e[:, None], axis=1)[:, 0] - 1
    counts = csum[-1]
    blocks_e = (counts + tm - 1) // tm
    blk_end = jnp.cumsum(blocks_e)
    blk_start = blk_end - blocks_e
    n_used = blk_end[-1:].astype(jnp.int32)
    dest = (blk_start[flat_e] * tm + rank).astype(jnp.int32)
    blk_e = jnp.clip(jnp.searchsorted(blk_end, jnp.arange(n_blk), side="right"),
                     0, e - 1).astype(jnp.int32)
    order = jnp.argsort(flat_e, stable=True)
    start = jnp.cumsum(counts) - counts
    r = jnp.arange(n_blk * tm)
    e_r = blk_e[r // tm]
    local = r - blk_start[e_r] * tm
    valid = (local < counts[e_r]) & (r // tm < n_used[0])
    a_r = order[jnp.clip(start[e_r] + local, 0, n_assign - 1)]
    row_tok = jnp.where(valid, a_r // TOP_K, 0).astype(jnp.int32)
    row_gate = jnp.where(valid, flat_gate[a_r], 0.0).astype(F32)[:, None]
    return dest, blk_e, n_used, row_tok, row_gate, n_blk


def _moe(h, w_router, b_router, w_up, b_up, w_down, b_down, ln_g, ln_b, cfg, alpha):
    n_tok = h.shape[0]
    top_idx, top_gate = _router(h, w_router, b_router, cfg)
    dest, blk_e, n_used, row_tok, row_gate, n_blk = _routing_tables(top_idx, top_gate, cfg)
    rows = _gather_rows(h, row_tok, n_used, n_blk, cfg)
    act = _expert_up(rows, w_up, b_up, blk_e, n_used, cfg)
    y_rows = _expert_down(act, w_down, b_down, row_gate, blk_e, n_used, cfg)
    tt = min(cfg.comb_tt, n_tok)
    dest_kmajor = dest.reshape(n_tok // tt, tt, TOP_K).transpose(0, 2, 1).reshape(-1)
    return _combine(y_rows, dest_kmajor, h, ln_g, ln_b, cfg, alpha)


def _layer(h, w_in, b_in, w_sb, w_fx, w_out, ln1_g, ln1_b, w_router, b_router,
           w_up, b_up, w_down, b_down, ln2_g, ln2_b, cfg):
    b, s, d = h.shape
    heads = cfg.heads
    width = heads * HEAD_DIM
    alpha = (2.0 * cfg.depth) ** 0.25
    n_tok = b * s
    x = h.reshape(n_tok, d)
    x_bf = x.astype(BF16)
    o_f = 6 * width
    o_g = o_f + heads
    qkv = _matmul(x_bf, w_in[:, :o_f].astype(BF16), b_in[None, :o_f], out_dtype=BF16, act=None,
                  tm=cfg.mm_tm, tn=cfg.mm_tn, name="in_proj_qkv")
    gates = _matmul(x_bf, w_in[:, o_g:].astype(BF16), b_in[None, o_g:], out_dtype=BF16,
                    act="sigmoid", tm=cfg.mm_tm, tn=cfg.mm_tn, name="in_proj_gates")
    w_f = jnp.zeros((d, LANES), BF16).at[:, :heads].set(w_in[:, o_f:o_g].astype(BF16))
    b_f = jnp.zeros((1, LANES), F32).at[0, :heads].set(b_in[o_f:o_g])
    f_logit = _matmul(x_bf, w_f, b_f, out_dtype=F32, act=None,
                      tm=cfg.mm_tm, tn=LANES, name="in_proj_forget")
    f_rows = f_logit[:, :heads].reshape(b, s, heads).transpose(0, 2, 1).reshape(b * heads, s)
    c_rows = _forget_cumsum(f_rows).reshape(b, heads, 1, s)

    qkv3 = qkv.reshape(b, s, o_f)
    o_sb = _attention(qkv3, None, cfg, "sb").reshape(n_tok, width)
    o_fx = _attention(qkv3, c_rows, cfg, "fx").reshape(n_tok, width)

    h1 = _post_attention(o_sb, o_fx, gates, w_sb.astype(BF16), w_fx.astype(BF16),
                         w_out.astype(BF16), x, ln1_g[None, :], ln1_b[None, :], cfg, alpha)
    out = _moe(h1, w_router, b_router, w_up, b_up, w_down, b_down,
               ln2_g[None, :], ln2_b[None, :], cfg, alpha)
    return out.reshape(b, s, d)


def kernel(x, w_in, b_in, w_branch_sb, w_branch_fx, w_out, ln1_g, ln1_b, w_router, b_router,
           w_up, b_up, w_down, b_down, ln2_g, ln2_b):
    depth = w_in.shape[0]
    d = x.shape[-1]
    n_experts = w_router.shape[-1]
    heads = (w_in.shape[-1] - 2 * d) // (6 * HEAD_DIM + 1)
    cfg = Config(heads=heads, n_experts=n_experts, depth=depth,
                 mm_tm=1024, mm_tn=1024, attn_tile=256, post_tm=256, post_tc=512,
                 route_tm=512, moe_tm=512, moe_tf=256, moe_tn=1024, comb_tt=128)
    h = x
    for l in range(depth):
        h = _layer(h, w_in[l], b_in[l], w_branch_sb[l], w_branch_fx[l], w_out[l],
                   ln1_g[l], ln1_b[l], w_router[l], b_router[l], w_up[l], b_up[l],
                   w_down[l], b_down[l], ln2_g[l], ln2_b[l], cfg)
    return h
```

```python
import functools
from typing import NamedTuple

import jax
import jax.numpy as jnp
from jax import lax
from jax.experimental import pallas as pl
from jax.experimental.pallas import tpu as pltpu

F32 = jnp.float32
BF16 = jnp.bfloat16

LANES = 128
HEAD_DIM = 128
TOP_K = 4
SWIGLU_ALPHA = 1.702
SWIGLU_LIMIT = 7.0
LN_EPS = 1e-5
LOG2_E = 1.4426950408889634
MASK_VALUE = -1e30
VMEM_LIMIT_BYTES = 60 * 1024 * 1024


class Config(NamedTuple):
    heads: int
    n_experts: int
    depth: int
    mm_tm: int
    mm_tn: int
    attn_tile: int
    attn_group: int
    post_tm: int
    post_tc: int
    route_tm: int
    moe_tm: int
    moe_tf: int
    moe_tn: int
    comb_tt: int


def _params(sem):
    return pltpu.CompilerParams(dimension_semantics=sem, vmem_limit_bytes=VMEM_LIMIT_BYTES)


def _matmul_kernel(x_ref, w_ref, b_ref, o_ref, *, act):
    acc = jnp.dot(x_ref[...], w_ref[...], preferred_element_type=F32) + b_ref[...]
    if act == "sigmoid":
        acc = jax.nn.sigmoid(acc)
    o_ref[...] = acc.astype(o_ref.dtype)


def _matmul(x, w, b, *, out_dtype, act, tm, tn, name):
    m, k = x.shape
    n = w.shape[1]
    tm, tn = min(tm, m), min(tn, n)
    return pl.pallas_call(
        functools.partial(_matmul_kernel, act=act),
        out_shape=jax.ShapeDtypeStruct((m, n), out_dtype),
        grid=(m // tm, n // tn),
        in_specs=[pl.BlockSpec((tm, k), lambda i, j: (i, 0)),
                  pl.BlockSpec((k, tn), lambda i, j: (0, j)),
                  pl.BlockSpec((1, tn), lambda i, j: (0, j))],
        out_specs=pl.BlockSpec((tm, tn), lambda i, j: (i, j)),
        compiler_params=_params(("parallel", "arbitrary")),
        name=name,
    )(x, w, b)


def _forget_cumsum_kernel(f_ref, c_ref):
    f = f_ref[...]
    x = jnp.minimum(f, 0.0) - jnp.log(1.0 + jnp.exp(-jnp.abs(f)))
    n = x.shape[1]
    lane = lax.broadcasted_iota(jnp.int32, x.shape, 1)
    d = 1
    while d < n:
        x = x + jnp.where(lane >= d, pltpu.roll(x, d, axis=1), 0.0)
        d *= 2
    c_ref[...] = x


def _forget_cumsum(f_logit_rows):
    return pl.pallas_call(
        _forget_cumsum_kernel,
        out_shape=jax.ShapeDtypeStruct(f_logit_rows.shape, F32),
        name="forget_cumsum",
    )(f_logit_rows)


def _qk(q, k):
    return lax.dot_general(q, k, (((1,), (1,)), ((), ())), preferred_element_type=F32)


def _lane_repeat(x, width):
    return jnp.concatenate([x] * (width // LANES), axis=1)


def _neg_abs(x):
    bits = pltpu.bitcast(x, jnp.uint32) | jnp.uint32(0x80000000)
    return pltpu.bitcast(bits, F32)


def _sb_attn_kernel(q_ref, k_ref, v_ref, o_ref, u_ref, acc_ref, carry_ref, z_ref, *, tile, scale,
                    group):
    i = pl.program_id(2)
    t = tile
    q = q_ref[0]
    row = lax.broadcasted_iota(jnp.int32, (t, t), 0)
    col = lax.broadcasted_iota(jnp.int32, (t, t), 1)
    strict = col < row
    u_ref[...] = (row > col).astype(BF16)
    acc_ref[...] = jnp.zeros_like(acc_ref)
    carry_ref[...] = jnp.zeros_like(carry_ref)

    heads = [slice(g * HEAD_DIM, (g + 1) * HEAD_DIM) for g in range(group)]

    def scores(j):
        kj = k_ref[0, pl.ds(pl.multiple_of(j * t, t), t), :]
        return [_qk(q[:, hd], kj[:, hd]) for hd in heads]

    def step(j, masked):
        z_next = scores(jnp.maximum(j - 1, 0))
        if masked:
            z_raw = scores(j)
        else:
            z_raw = [z_ref[:, g * t:(g + 1) * t] for g in range(group)]
        vj = v_ref[0, pl.ds(pl.multiple_of(j * t, t), t), :]
        u = u_ref[...]
        carry = carry_ref[...]
        z = [zg * (scale * LOG2_E) for zg in z_raw]
        log_beta, log_keep, row_sum = [], [], []
        for zg in z:
            lb = jnp.minimum(zg, 0.0) - jnp.log2(1.0 + jnp.exp2(_neg_abs(zg)))
            lk = lb - zg
            if masked:
                lk = jnp.where(strict, lk, 0.0)
            log_beta.append(lb)
            log_keep.append(lk.astype(BF16))
            row_sum.append(jnp.broadcast_to(jnp.sum(lk, axis=1, keepdims=True), (t, LANES)))
        after = [jnp.dot(lk, u, preferred_element_type=F32) for lk in log_keep]
        w = []
        for hd, lb, af in zip(heads, log_beta, after):
            wg = jnp.exp2(lb + af + _lane_repeat(carry[:, hd], t))
            if masked:
                wg = jnp.where(strict, wg, 0.0)
            w.append(wg.astype(BF16))
        pv = [jnp.dot(wg, vj[:, hd], preferred_element_type=F32) for hd, wg in zip(heads, w)]
        acc_ref[...] += jnp.concatenate(pv, axis=1)
        carry_ref[...] = carry + jnp.concatenate(row_sum, axis=1)
        z_ref[...] = jnp.concatenate(z_next, axis=1)

    step(i, True)

    def body(it, c):
        step(i - 1 - it, False)
        return c

    lax.fori_loop(0, i, body, 0)
    o_ref[0] = acc_ref[...].astype(o_ref.dtype)


def _fx_attn_kernel(q_ref, k_ref, v_ref, c_ref, o_ref, m_ref, acc_ref, z_ref, *, tile, scale,
                    group):
    i = pl.program_id(2)
    t = tile
    q = q_ref[0]
    row = lax.broadcasted_iota(jnp.int32, (t, t), 0)
    col = lax.broadcasted_iota(jnp.int32, (t, t), 1)
    causal = col <= row
    m_ref[...] = jnp.full_like(m_ref, MASK_VALUE)
    acc_ref[...] = jnp.zeros_like(acc_ref)
    ones = jnp.ones((t, LANES), BF16)
    heads = [slice(g * HEAD_DIM, (g + 1) * HEAD_DIM) for g in range(group)]

    def scores(j):
        kj = k_ref[0, pl.ds(pl.multiple_of(j * t, t), t), :]
        return jnp.concatenate([_qk(q[:, hd], kj[:, hd]) for hd in heads], axis=1)

    def step(j, masked):
        if not masked:
            z_next = scores(j + 1)
        start = pl.multiple_of(j * t, t)
        vj = v_ref[0, pl.ds(start, t), :]
        m_all = m_ref[...]
        p, rescale, m_new = [], [], []
        for g, hd in enumerate(heads):
            s = (z_ref[:, g * t:(g + 1) * t] * (scale * LOG2_E)
                 - c_ref[0, g, :, pl.ds(start, t)] * LOG2_E)
            if masked:
                s = jnp.where(causal, s, MASK_VALUE)
            m_prev = m_all[:, hd]
            m_next = jnp.maximum(m_prev, jnp.max(s, axis=1, keepdims=True))
            p.append(jnp.exp2(s - _lane_repeat(m_next, t)).astype(BF16))
            rescale.append(_lane_repeat(jnp.exp2(m_prev - m_next), 2 * LANES))
            m_new.append(m_next)
        pv = [jnp.dot(pg, jnp.concatenate([vj[:, hd], ones], axis=1), preferred_element_type=F32)
              for hd, pg in zip(heads, p)]
        acc_ref[...] = jnp.concatenate(rescale, axis=1) * acc_ref[...] + jnp.concatenate(pv, axis=1)
        m_ref[...] = jnp.concatenate(m_new, axis=1)
        if not masked:
            z_ref[...] = z_next

    def body(j, c):
        step(j, False)
        return c

    z_ref[...] = scores(0)
    lax.fori_loop(0, i, body, 0)
    step(i, True)
    for g in range(group):
        acc = acc_ref[:, 2 * g * LANES:2 * (g + 1) * LANES]
        o_ref[0, :, g * HEAD_DIM:(g + 1) * HEAD_DIM] = (
            acc[:, :LANES] / acc[:, LANES:]).astype(o_ref.dtype)


def _attention(qkv, c_rows, cfg, which):
    b, s, _ = qkv.shape
    h = cfg.heads
    grp = cfg.attn_group
    ng = h // grp
    t = min(cfg.attn_tile, s)
    base = 0 if which == "sb" else 3 * ng
    scale = HEAD_DIM ** -0.5
    gw = grp * HEAD_DIM
    q_spec = pl.BlockSpec((1, t, gw), lambda bi, hi, qi: (bi, qi, base + hi))
    k_spec = pl.BlockSpec((1, s, gw), lambda bi, hi, qi: (bi, 0, base + ng + hi))
    v_spec = pl.BlockSpec((1, s, gw), lambda bi, hi, qi: (bi, 0, base + 2 * ng + hi))
    o_spec = pl.BlockSpec((1, t, gw), lambda bi, hi, qi: (bi, qi, hi))
    out_shape = jax.ShapeDtypeStruct((b, s, h * HEAD_DIM), BF16)
    grid = (b, ng, s // t)
    sem = ("parallel", "parallel", "arbitrary")
    if which == "sb":
        return pl.pallas_call(
            functools.partial(_sb_attn_kernel, tile=t, scale=scale, group=grp),
            out_shape=out_shape, grid=grid,
            in_specs=[q_spec, k_spec, v_spec], out_specs=o_spec,
            scratch_shapes=[pltpu.VMEM((t, t), BF16),
                            pltpu.VMEM((t, gw), F32),
                            pltpu.VMEM((t, grp * LANES), F32),
                            pltpu.VMEM((t, grp * t), F32)],
            compiler_params=_params(sem), name="sb_attention",
        )(qkv, qkv, qkv)
    c_spec = pl.BlockSpec((1, grp, 1, s), lambda bi, hi, qi: (bi, hi, 0, 0))
    return pl.pallas_call(
        functools.partial(_fx_attn_kernel, tile=t, scale=scale, group=grp),
        out_shape=out_shape, grid=grid,
        in_specs=[q_spec, k_spec, v_spec, c_spec], out_specs=o_spec,
        scratch_shapes=[pltpu.VMEM((t, grp * LANES), F32),
                        pltpu.VMEM((t, grp * 2 * LANES), F32),
                        pltpu.VMEM((t, grp * t), F32)],
        compiler_params=_params(sem), name="fx_attention",
    )(qkv, qkv, qkv, c_rows)


def _layer_norm(y, g, b):
    mu = jnp.mean(y, axis=-1, keepdims=True)
    d = y - mu
    var = jnp.mean(d * d, axis=-1, keepdims=True)
    return d * lax.rsqrt(var + LN_EPS) * g + b


def _post_attn_kernel(osb_ref, ofx_ref, gsb_ref, gfx_ref, wsb_ref, wfx_ref, wout_ref,
                      x_hbm, g_ref, b_ref, h_ref, x_buf, sem, *, alpha, tm):
    i = pl.program_id(0)
    c = pl.program_id(1)
    x_copy = pltpu.make_async_copy(x_hbm.at[pl.ds(i * tm, tm), :], x_buf, sem)

    @pl.when(c == 0)
    def _():
        x_copy.start()
        h_ref[...] = jnp.zeros_like(h_ref)

    tsb = jnp.dot(osb_ref[...], wsb_ref[...], preferred_element_type=F32)
    tfx = jnp.dot(ofx_ref[...], wfx_ref[...], preferred_element_type=F32)
    merged = gsb_ref[...].astype(F32) * tsb + gfx_ref[...].astype(F32) * tfx
    h_ref[...] += jnp.dot(merged.astype(BF16), wout_ref[...], preferred_element_type=F32)

    @pl.when(c == pl.num_programs(1) - 1)
    def _():
        x_copy.wait()
        h_ref[...] = _layer_norm(alpha * x_buf[...] + h_ref[...], g_ref[...], b_ref[...])


def _post_attention(o_sb, o_fx, gates, w_sb, w_fx, w_out, x, ln_g, ln_b, cfg, alpha):
    m, d = x.shape
    width = o_sb.shape[1]
    tm, tc = min(cfg.post_tm, m), min(cfg.post_tc, d)
    nc = d // tc
    return pl.pallas_call(
        functools.partial(_post_attn_kernel, alpha=alpha, tm=tm),
        out_shape=jax.ShapeDtypeStruct((m, d), F32),
        grid=(m // tm, nc),
        in_specs=[pl.BlockSpec((tm, width), lambda i, c: (i, 0)),
                  pl.BlockSpec((tm, width), lambda i, c: (i, 0)),
                  pl.BlockSpec((tm, tc), lambda i, c: (i, c)),
                  pl.BlockSpec((tm, tc), lambda i, c: (i, nc + c)),
                  pl.BlockSpec((width, tc), lambda i, c: (0, c)),
                  pl.BlockSpec((width, tc), lambda i, c: (0, c)),
                  pl.BlockSpec((tc, d), lambda i, c: (c, 0)),
                  pl.BlockSpec(memory_space=pl.ANY),
                  pl.BlockSpec((1, d), lambda i, c: (0, 0)),
                  pl.BlockSpec((1, d), lambda i, c: (0, 0))],
        out_specs=pl.BlockSpec((tm, d), lambda i, c: (i, 0)),
        scratch_shapes=[pltpu.VMEM((tm, d), F32), pltpu.SemaphoreType.DMA],
        compiler_params=_params(("arbitrary", "arbitrary")),
        name="post_attention",
    )(o_sb, o_fx, gates, gates, w_sb, w_fx, w_out, x, ln_g, ln_b)


def _router_kernel(h_ref, w_ref, b_ref, idx_ref, gate_ref):
    logits = jnp.dot(h_ref[...], w_ref[...], preferred_element_type=F32,
                     precision=lax.Precision.HIGHEST) + b_ref[...]
    lane = lax.broadcasted_iota(jnp.int32, logits.shape, 1)
    lane_f = lane.astype(F32)
    vals = logits
    idx_out = jnp.zeros(logits.shape, F32)
    exp_out = jnp.zeros(logits.shape, F32)
    denom = jnp.zeros((logits.shape[0], 1), F32)
    top = None
    for k in range(TOP_K):
        m = jnp.max(vals, axis=1, keepdims=True)
        sel = jnp.min(jnp.where(vals == m, lane_f, float(LANES)), axis=1, keepdims=True)
        if k == 0:
            top = m
        e = jnp.exp(m - top)
        denom = denom + e
        idx_out = jnp.where(lane == k, sel, idx_out)
        exp_out = jnp.where(lane == k, e, exp_out)
        vals = jnp.where(lane_f == sel, MASK_VALUE * 2.0, vals)
    idx_ref[...] = idx_out.astype(jnp.int32)
    gate_ref[...] = exp_out / denom


def _router(h, w_router, b_router, cfg):
    m, d = h.shape
    e = cfg.n_experts
    tm = min(cfg.route_tm, m)
    w_pad = jnp.zeros((d, LANES), F32).at[:, :e].set(w_router)
    b_pad = jnp.full((1, LANES), MASK_VALUE, F32).at[0, :e].set(b_router)
    return pl.pallas_call(
        _router_kernel,
        out_shape=(jax.ShapeDtypeStruct((m, LANES), jnp.int32),
                   jax.ShapeDtypeStruct((m, LANES), F32)),
        grid=(m // tm,),
        in_specs=[pl.BlockSpec((tm, d), lambda i: (i, 0)),
                  pl.BlockSpec((d, LANES), lambda i: (0, 0)),
                  pl.BlockSpec((1, LANES), lambda i: (0, 0))],
        out_specs=(pl.BlockSpec((tm, LANES), lambda i: (i, 0)),
                   pl.BlockSpec((tm, LANES), lambda i: (i, 0))),
        compiler_params=_params(("parallel",)),
        name="router",
    )(h, w_pad, b_pad)


def _row_copy(src_hbm, row, dst_vmem, slot, sem):
    return pltpu.make_async_copy(src_hbm.at[pl.ds(row, 1), :], dst_vmem.at[pl.ds(slot, 1), :], sem)


def _gather_rows_kernel(tok_ref, src_ref, nvalid_ref, h_hbm, o_ref, buf_ref, sem, *, tm):
    blk = pl.program_id(0)

    def copy(b, r):
        slot = b % 2
        return _row_copy(h_hbm, tok_ref[src_ref[b] + r], buf_ref.at[slot], r, sem.at[slot])

    def issue(b):
        def body(r, c):
            copy(b, r).start()
            return c
        lax.fori_loop(0, nvalid_ref[b], body, 0)

    @pl.when(blk == 0)
    def _():
        buf_ref[...] = jnp.zeros_like(buf_ref)
        issue(0)

    @pl.when(blk + 1 < pl.num_programs(0))
    def _():
        issue(blk + 1)

    def drain(r, c):
        copy(blk, r).wait()
        return c

    lax.fori_loop(0, nvalid_ref[blk], drain, 0)
    rows = buf_ref[blk % 2]
    valid = lax.broadcasted_iota(jnp.int32, (tm, 1), 0) < nvalid_ref[blk]
    o_ref[...] = jnp.where(valid, rows, 0.0).astype(o_ref.dtype)


def _gather_rows(h, sorted_tok, blk_src, blk_nvalid, cfg):
    d = h.shape[1]
    tm = cfg.moe_tm
    n_blk = blk_src.shape[0]
    return pl.pallas_call(
        functools.partial(_gather_rows_kernel, tm=tm),
        out_shape=jax.ShapeDtypeStruct((n_blk * tm, d), BF16),
        grid_spec=pltpu.PrefetchScalarGridSpec(
            num_scalar_prefetch=3, grid=(n_blk,),
            in_specs=[pl.BlockSpec(memory_space=pl.ANY)],
            out_specs=pl.BlockSpec((tm, d), lambda i, tok, src, nv: (i, 0)),
            scratch_shapes=[pltpu.VMEM((2, tm, d), F32), pltpu.SemaphoreType.DMA((2,))]),
        compiler_params=_params(("arbitrary",)),
        name="moe_gather",
    )(sorted_tok, blk_src, blk_nvalid, h)


def _expert_up_kernel(blk_e_ref, nused_ref, x_ref, wg_ref, wu_ref, bg_ref, bu_ref, act_ref,
                      wg_bf, wu_bf):
    blk = pl.program_id(1)
    e = blk_e_ref[blk]
    e_prev = blk_e_ref[jnp.maximum(blk - 1, 0)]

    @pl.when((blk == 0) | (e != e_prev))
    def _():
        wg_bf[...] = wg_ref[0].astype(BF16)
        wu_bf[...] = wu_ref[0].astype(BF16)

    @pl.when(blk < nused_ref[0])
    def _():
        x = x_ref[...]
        g = jnp.dot(x, wg_bf[...], preferred_element_type=F32) + bg_ref[0]
        u = jnp.dot(x, wu_bf[...], preferred_element_type=F32) + bu_ref[0]
        g = jnp.minimum(g, SWIGLU_LIMIT)
        u = jnp.clip(u, -SWIGLU_LIMIT, SWIGLU_LIMIT)
        act_ref[...] = (g * jax.nn.sigmoid(SWIGLU_ALPHA * g) * (u + 1.0)).astype(act_ref.dtype)

    @pl.when(blk >= nused_ref[0])
    def _():
        act_ref[...] = jnp.zeros_like(act_ref)


def _expert_up(rows, w_up, b_up, blk_e, n_used, cfg):
    n_rows, d = rows.shape
    d_ff = w_up.shape[2] // 2
    tm, tf = cfg.moe_tm, min(cfg.moe_tf, d_ff)
    nf = d_ff // tf
    n_blk = n_rows // tm
    b_up3 = b_up.reshape(b_up.shape[0], 1, 2 * d_ff)
    return pl.pallas_call(
        _expert_up_kernel,
        out_shape=jax.ShapeDtypeStruct((n_rows, d_ff), BF16),
        grid_spec=pltpu.PrefetchScalarGridSpec(
            num_scalar_prefetch=2, grid=(nf, n_blk),
            in_specs=[pl.BlockSpec((tm, d), lambda f, i, be, nu: (i, 0)),
                      pl.BlockSpec((1, d, tf), lambda f, i, be, nu: (be[i], 0, f)),
                      pl.BlockSpec((1, d, tf), lambda f, i, be, nu: (be[i], 0, nf + f)),
                      pl.BlockSpec((1, 1, tf), lambda f, i, be, nu: (be[i], 0, f)),
                      pl.BlockSpec((1, 1, tf), lambda f, i, be, nu: (be[i], 0, nf + f))],
            out_specs=pl.BlockSpec((tm, tf), lambda f, i, be, nu: (i, f)),
            scratch_shapes=[pltpu.VMEM((d, tf), BF16), pltpu.VMEM((d, tf), BF16)]),
        compiler_params=_params(("arbitrary", "arbitrary")),
        name="moe_up",
    )(blk_e, n_used, rows, w_up, w_up, b_up3, b_up3)


def _expert_down_kernel(blk_e_ref, nused_ref, a_ref, w_ref, b_ref, y_ref, w_bf):
    blk = pl.program_id(1)
    e = blk_e_ref[blk]
    e_prev = blk_e_ref[jnp.maximum(blk - 1, 0)]

    @pl.when((blk == 0) | (e != e_prev))
    def _():
        w_bf[...] = w_ref[0].astype(BF16)

    @pl.when(blk < nused_ref[0])
    def _():
        y_ref[...] = jnp.dot(a_ref[...], w_bf[...], preferred_element_type=F32) + b_ref[0]

    @pl.when(blk >= nused_ref[0])
    def _():
        y_ref[...] = jnp.zeros_like(y_ref)


def _expert_down(act, w_down, b_down, blk_e, n_used, cfg):
    n_rows, d_ff = act.shape
    d = w_down.shape[2]
    tm, tn = cfg.moe_tm, min(cfg.moe_tn, d)
    n_blk = n_rows // tm
    b_down3 = b_down.reshape(b_down.shape[0], 1, d)
    return pl.pallas_call(
        _expert_down_kernel,
        out_shape=jax.ShapeDtypeStruct((n_rows, d), F32),
        grid_spec=pltpu.PrefetchScalarGridSpec(
            num_scalar_prefetch=2, grid=(d // tn, n_blk),
            in_specs=[pl.BlockSpec((tm, d_ff), lambda n, i, be, nu: (i, 0)),
                      pl.BlockSpec((1, d_ff, tn), lambda n, i, be, nu: (be[i], 0, n)),
                      pl.BlockSpec((1, 1, tn), lambda n, i, be, nu: (be[i], 0, n))],
            out_specs=pl.BlockSpec((tm, tn), lambda n, i, be, nu: (i, n)),
            scratch_shapes=[pltpu.VMEM((d_ff, tn), BF16)]),
        compiler_params=_params(("arbitrary", "arbitrary")),
        name="moe_down",
    )(blk_e, n_used, act, w_down, b_down3)


def _combine_kernel(dest_ref, y_hbm, gate_ref, h_ref, g_ref, b_ref, o_ref, buf_ref, sem,
                    *, tt, alpha):
    i = pl.program_id(0)
    n_copy = tt * TOP_K

    def copy(tile, a):
        slot = tile % 2
        return _row_copy(y_hbm, dest_ref[tile * n_copy + a], buf_ref.at[slot], a, sem.at[slot])

    def issue(tile):
        def body(a, c):
            copy(tile, a).start()
            return c
        lax.fori_loop(0, n_copy, body, 0)

    @pl.when(i == 0)
    def _():
        issue(0)

    @pl.when(i + 1 < pl.num_programs(0))
    def _():
        issue(i + 1)

    def drain(a, c):
        copy(i, a).wait()
        return c

    lax.fori_loop(0, n_copy, drain, 0)
    slot = i % 2
    gate = gate_ref[...]
    ffn = gate[:, 0:1] * buf_ref[slot, pl.ds(0, tt), :]
    for k in range(1, TOP_K):
        ffn = ffn + gate[:, k:k + 1] * buf_ref[slot, pl.ds(k * tt, tt), :]
    o_ref[...] = _layer_norm(alpha * h_ref[...] + ffn, g_ref[...], b_ref[...])


def _combine(y_rows, dest_kmajor, gate, h, ln_g, ln_b, cfg, alpha):
    m, d = h.shape
    tt = min(cfg.comb_tt, m)
    return pl.pallas_call(
        functools.partial(_combine_kernel, tt=tt, alpha=alpha),
        out_shape=jax.ShapeDtypeStruct((m, d), F32),
        grid_spec=pltpu.PrefetchScalarGridSpec(
            num_scalar_prefetch=1, grid=(m // tt,),
            in_specs=[pl.BlockSpec(memory_space=pl.ANY),
                      pl.BlockSpec((tt, LANES), lambda i, dst: (i, 0)),
                      pl.BlockSpec((tt, d), lambda i, dst: (i, 0)),
                      pl.BlockSpec((1, d), lambda i, dst: (0, 0)),
                      pl.BlockSpec((1, d), lambda i, dst: (0, 0))],
            out_specs=pl.BlockSpec((tt, d), lambda i, dst: (i, 0)),
            scratch_shapes=[pltpu.VMEM((2, tt * TOP_K, d), F32), pltpu.SemaphoreType.DMA((2,))]),
        compiler_params=_params(("arbitrary",)),
        name="moe_combine",
    )(dest_kmajor, y_rows, gate, h, ln_g, ln_b)


def _routing_tables(top_idx, cfg):
    n_tok = top_idx.shape[0]
    e, tm = cfg.n_experts, cfg.moe_tm
    n_assign = n_tok * TOP_K
    n_blk = -(-n_assign // tm) + e
    flat_e = top_idx.reshape(-1)
    assign = jnp.arange(n_assign, dtype=jnp.int32)
    sorted_e, sorted_a = lax.sort((flat_e, assign), num_keys=2)
    experts = jnp.arange(e, dtype=jnp.int32)
    onehot = sorted_e[:, None] == experts[None, :]
    counts = jnp.sum(onehot, axis=0, dtype=jnp.int32)
    start = jnp.cumsum(counts) - counts
    blocks_e = (counts + tm - 1) // tm
    blk_end = jnp.cumsum(blocks_e)
    blk_start = blk_end - blocks_e
    shift = blk_start * tm - start
    dest_sorted = assign + jnp.sum(jnp.where(onehot, shift[None, :], 0), axis=1, dtype=jnp.int32)
    _, dest = lax.sort((sorted_a, dest_sorted), num_keys=1)
    blk = jnp.arange(n_blk, dtype=jnp.int32)
    blk_onehot = (blk[:, None] >= blk_start[None, :]) & (blk[:, None] < blk_end[None, :])
    blk_e = jnp.sum(jnp.where(blk_onehot, experts[None, :], 0), axis=1, dtype=jnp.int32)
    n_used = blk_end[-1:].astype(jnp.int32)
    blk_e = jnp.where(blk < n_used[0], blk_e, e - 1)
    local = (blk - jnp.sum(jnp.where(blk_onehot, blk_start[None, :], 0), axis=1)) * tm
    blk_count = jnp.sum(jnp.where(blk_onehot, counts[None, :], 0), axis=1)
    blk_src = (jnp.sum(jnp.where(blk_onehot, start[None, :], 0), axis=1) + local).astype(jnp.int32)
    blk_nvalid = jnp.clip(blk_count - local, 0, tm).astype(jnp.int32)
    sorted_tok = sorted_a // TOP_K
    return dest.astype(jnp.int32), blk_e, n_used, sorted_tok, blk_src, blk_nvalid


def _moe(h, w_router, b_router, w_up, b_up, w_down, b_down, ln_g, ln_b, cfg, alpha):
    n_tok = h.shape[0]
    idx_pad, gate_pad = _router(h, w_router, b_router, cfg)
    dest, blk_e, n_used, sorted_tok, blk_src, blk_nvalid = _routing_tables(idx_pad[:, :TOP_K], cfg)
    rows = _gather_rows(h, sorted_tok, blk_src, blk_nvalid, cfg)
    act = _expert_up(rows, w_up, b_up, blk_e, n_used, cfg)
    y_rows = _expert_down(act, w_down, b_down, blk_e, n_used, cfg)
    tt = min(cfg.comb_tt, n_tok)
    dest_kmajor = dest.reshape(n_tok // tt, tt, TOP_K).transpose(0, 2, 1).reshape(-1)
    return _combine(y_rows, dest_kmajor, gate_pad, h, ln_g, ln_b, cfg, alpha)


def _layer(h, w_in, b_in, w_sb, w_fx, w_out, ln1_g, ln1_b, w_router, b_router,
           w_up, b_up, w_down, b_down, ln2_g, ln2_b, cfg):
    b, s, d = h.shape
    heads = cfg.heads
    width = heads * HEAD_DIM
    alpha = (2.0 * cfg.depth) ** 0.25
    n_tok = b * s
    x = h.reshape(n_tok, d)
    x_bf = x.astype(BF16)
    o_f = 6 * width
    o_g = o_f + heads
    qkv = _matmul(x_bf, w_in[:, :o_f].astype(BF16), b_in[None, :o_f], out_dtype=BF16, act=None,
                  tm=cfg.mm_tm, tn=cfg.mm_tn, name="in_proj_qkv")
    gates = _matmul(x_bf, w_in[:, o_g:].astype(BF16), b_in[None, o_g:], out_dtype=BF16,
                    act="sigmoid", tm=cfg.mm_tm, tn=cfg.mm_tn, name="in_proj_gates")
    w_f = jnp.zeros((d, LANES), BF16).at[:, :heads].set(w_in[:, o_f:o_g].astype(BF16))
    b_f = jnp.zeros((1, LANES), F32).at[0, :heads].set(b_in[o_f:o_g])
    f_logit = _matmul(x_bf, w_f, b_f, out_dtype=F32, act=None,
                      tm=cfg.mm_tm, tn=LANES, name="in_proj_forget")
    f_rows = f_logit[:, :heads].reshape(b, s, heads).transpose(0, 2, 1).reshape(b * heads, s)
    c_rows = _forget_cumsum(f_rows).reshape(b, heads, 1, s)

    qkv3 = qkv.reshape(b, s, o_f)
    o_sb = _attention(qkv3, None, cfg, "sb").reshape(n_tok, width)
    o_fx = _attention(qkv3, c_rows, cfg, "fx").reshape(n_tok, width)

    h1 = _post_attention(o_sb, o_fx, gates, w_sb.astype(BF16), w_fx.astype(BF16),
                         w_out.astype(BF16), x, ln1_g[None, :], ln1_b[None, :], cfg, alpha)
    out = _moe(h1, w_router, b_router, w_up, b_up, w_down, b_down,
               ln2_g[None, :], ln2_b[None, :], cfg, alpha)
    return out.reshape(b, s, d)


def kernel(x, w_in, b_in, w_branch_sb, w_branch_fx, w_out, ln1_g, ln1_b, w_router, b_router,
           w_up, b_up, w_down, b_down, ln2_g, ln2_b):
    depth = w_in.shape[0]
    d = x.shape[-1]
    n_experts = w_router.shape[-1]
    heads = (w_in.shape[-1] - 2 * d) // (6 * HEAD_DIM + 1)
    cfg = Config(heads=heads, n_experts=n_experts, depth=depth,
                 mm_tm=1024, mm_tn=1024, attn_tile=256, attn_group=4, post_tm=512, post_tc=512,
                 route_tm=512, moe_tm=512, moe_tf=512, moe_tn=2048, comb_tt=128)
    h = x
    for l in range(depth):
        h = _layer(h, w_in[l], b_in[l], w_branch_sb[l], w_branch_fx[l], w_out[l],
                   ln1_g[l], ln1_b[l], w_router[l], b_router[l], w_up[l], b_up[l],
                   w_down[l], b_down[l], ln2_g[l], ln2_b[l], cfg)
    return h
```

```python
import functools
from typing import NamedTuple

import jax
import jax.numpy as jnp
from jax import lax
from jax.experimental import pallas as pl
from jax.experimental.pallas import tpu as pltpu

F32 = jnp.float32
BF16 = jnp.bfloat16

LANES = 128
HEAD_DIM = 128
TOP_K = 4
SWIGLU_ALPHA = 1.702
SWIGLU_LIMIT = 7.0
LN_EPS = 1e-5
LOG2_E = 1.4426950408889634
MASK_VALUE = -1e30
SB_EXIT_LOG2 = -160.0
VMEM_LIMIT_BYTES = 60 * 1024 * 1024


class Config(NamedTuple):
    heads: int
    n_experts: int
    depth: int
    mm_tm: int
    mm_tn: int
    attn_tile: int
    attn_group: int
    post_tm: int
    post_tc: int
    route_tm: int
    moe_tm: int
    moe_tf: int
    moe_tn: int
    comb_tt: int


def _params(sem):
    return pltpu.CompilerParams(dimension_semantics=sem, vmem_limit_bytes=VMEM_LIMIT_BYTES)


def _matmul_kernel(x_ref, w_ref, b_ref, o_ref, *, act):
    acc = jnp.dot(x_ref[...], w_ref[...], preferred_element_type=F32) + b_ref[...]
    if act == "sigmoid":
        acc = jax.nn.sigmoid(acc)
    o_ref[...] = acc.astype(o_ref.dtype)


def _matmul(x, w, b, *, out_dtype, act, tm, tn, name):
    m, k = x.shape
    n = w.shape[1]
    tm, tn = min(tm, m), min(tn, n)
    return pl.pallas_call(
        functools.partial(_matmul_kernel, act=act),
        out_shape=jax.ShapeDtypeStruct((m, n), out_dtype),
        grid=(m // tm, n // tn),
        in_specs=[pl.BlockSpec((tm, k), lambda i, j: (i, 0)),
                  pl.BlockSpec((k, tn), lambda i, j: (0, j)),
                  pl.BlockSpec((1, tn), lambda i, j: (0, j))],
        out_specs=pl.BlockSpec((tm, tn), lambda i, j: (i, j)),
        compiler_params=_params(("parallel", "arbitrary")),
        name=name,
    )(x, w, b)


def _forget_cumsum_kernel(f_ref, c_ref):
    f = f_ref[...]
    x = jnp.minimum(f, 0.0) - jnp.log(1.0 + jnp.exp(-jnp.abs(f)))
    n = x.shape[1]
    lane = lax.broadcasted_iota(jnp.int32, x.shape, 1)
    d = 1
    while d < n:
        x = x + jnp.where(lane >= d, pltpu.roll(x, d, axis=1), 0.0)
        d *= 2
    c_ref[...] = x


def _forget_cumsum(f_logit_rows):
    return pl.pallas_call(
        _forget_cumsum_kernel,
        out_shape=jax.ShapeDtypeStruct(f_logit_rows.shape, F32),
        name="forget_cumsum",
    )(f_logit_rows)


def _qk(q, k):
    return lax.dot_general(q, k, (((1,), (1,)), ((), ())), preferred_element_type=F32)


def _lane_repeat(x, width):
    return jnp.concatenate([x] * (width // LANES), axis=1)


def _neg_abs(x):
    bits = pltpu.bitcast(x, jnp.uint32) | jnp.uint32(0x80000000)
    return pltpu.bitcast(bits, F32)


def _sb_attn_kernel(q_ref, k_ref, v_ref, o_ref, u_ref, acc_ref, carry_ref, z_ref, *, tile, scale,
                    group):
    i = pl.program_id(2)
    t = tile
    q = q_ref[0]
    row = lax.broadcasted_iota(jnp.int32, (t, t), 0)
    col = lax.broadcasted_iota(jnp.int32, (t, t), 1)
    strict = col < row
    u_ref[...] = (row > col).astype(BF16)
    acc_ref[...] = jnp.zeros_like(acc_ref)
    carry_ref[...] = jnp.zeros_like(carry_ref)

    heads = [slice(g * HEAD_DIM, (g + 1) * HEAD_DIM) for g in range(group)]

    def scores(j):
        kj = k_ref[0, pl.ds(pl.multiple_of(j * t, t), t), :]
        return [_qk(q[:, hd], kj[:, hd]) for hd in heads]

    def step(j, masked):
        z_next = scores(jnp.maximum(j - 1, 0))
        if masked:
            z_raw = scores(j)
        else:
            z_raw = [z_ref[:, g * t:(g + 1) * t] for g in range(group)]
        vj = v_ref[0, pl.ds(pl.multiple_of(j * t, t), t), :]
        u = u_ref[...]
        carry = carry_ref[...]
        z = [zg * (scale * LOG2_E) for zg in z_raw]
        log_beta, log_keep, row_sum = [], [], []
        for zg in z:
            lb = jnp.minimum(zg, 0.0) - jnp.log2(1.0 + jnp.exp2(_neg_abs(zg)))
            lk = lb - zg
            if masked:
                lk = jnp.where(strict, lk, 0.0)
            log_beta.append(lb)
            log_keep.append(lk.astype(BF16))
            row_sum.append(jnp.broadcast_to(jnp.sum(lk, axis=1, keepdims=True), (t, LANES)))
        after = [jnp.dot(lk, u, preferred_element_type=F32) for lk in log_keep]
        w = []
        for hd, lb, af in zip(heads, log_beta, after):
            wg = jnp.exp2(lb + af + _lane_repeat(carry[:, hd], t))
            if masked:
                wg = jnp.where(strict, wg, 0.0)
            w.append(wg.astype(BF16))
        pv = [jnp.dot(wg, vj[:, hd], preferred_element_type=F32) for hd, wg in zip(heads, w)]
        acc_ref[...] += jnp.concatenate(pv, axis=1)
        carry_ref[...] = carry + jnp.concatenate(row_sum, axis=1)
        z_ref[...] = jnp.concatenate(z_next, axis=1)

    step(i, True)

    def alive():
        return jnp.max(carry_ref[...]) > SB_EXIT_LOG2

    def cond(state):
        it, go = state
        return (it < i) & go

    def body(state):
        it, _ = state
        step(i - 1 - it, False)
        return it + 1, alive()

    lax.while_loop(cond, body, (jnp.int32(0), alive()))
    o_ref[0] = acc_ref[...].astype(o_ref.dtype)


def _fx_attn_kernel(q_ref, k_ref, v_ref, c_ref, o_ref, m_ref, acc_ref, z_ref, *, tile, scale,
                    group):
    i = pl.program_id(2)
    t = tile
    q = q_ref[0]
    row = lax.broadcasted_iota(jnp.int32, (t, t), 0)
    col = lax.broadcasted_iota(jnp.int32, (t, t), 1)
    causal = col <= row
    m_ref[...] = jnp.full_like(m_ref, MASK_VALUE)
    acc_ref[...] = jnp.zeros_like(acc_ref)
    ones = jnp.ones((t, LANES), BF16)
    heads = [slice(g * HEAD_DIM, (g + 1) * HEAD_DIM) for g in range(group)]

    def scores(j):
        kj = k_ref[0, pl.ds(pl.multiple_of(j * t, t), t), :]
        return jnp.concatenate([_qk(q[:, hd], kj[:, hd]) for hd in heads], axis=1)

    def step(j, masked):
        if not masked:
            z_next = scores(j + 1)
        start = pl.multiple_of(j * t, t)
        vj = v_ref[0, pl.ds(start, t), :]
        m_all = m_ref[...]
        p, rescale, m_new = [], [], []
        for g, hd in enumerate(heads):
            s = (z_ref[:, g * t:(g + 1) * t] * (scale * LOG2_E)
                 - c_ref[0, g, :, pl.ds(start, t)] * LOG2_E)
            if masked:
                s = jnp.where(causal, s, MASK_VALUE)
            m_prev = m_all[:, hd]
            m_next = jnp.maximum(m_prev, jnp.max(s, axis=1, keepdims=True))
            p.append(jnp.exp2(s - _lane_repeat(m_next, t)).astype(BF16))
            rescale.append(_lane_repeat(jnp.exp2(m_prev - m_next), 2 * LANES))
            m_new.append(m_next)
        pv = [jnp.dot(pg, jnp.concatenate([vj[:, hd], ones], axis=1), preferred_element_type=F32)
              for hd, pg in zip(heads, p)]
        acc_ref[...] = jnp.concatenate(rescale, axis=1) * acc_ref[...] + jnp.concatenate(pv, axis=1)
        m_ref[...] = jnp.concatenate(m_new, axis=1)
        if not masked:
            z_ref[...] = z_next

    def body(j, c):
        step(j, False)
        return c

    z_ref[...] = scores(0)
    lax.fori_loop(0, i, body, 0)
    step(i, True)
    for g in range(group):
        acc = acc_ref[:, 2 * g * LANES:2 * (g + 1) * LANES]
        o_ref[0, :, g * HEAD_DIM:(g + 1) * HEAD_DIM] = (
            acc[:, :LANES] / acc[:, LANES:]).astype(o_ref.dtype)


def _attention(qkv, c_rows, cfg, which):
    b, s, _ = qkv.shape
    h = cfg.heads
    grp = cfg.attn_group
    ng = h // grp
    t = min(cfg.attn_tile, s)
    base = 0 if which == "sb" else 3 * ng
    scale = HEAD_DIM ** -0.5
    gw = grp * HEAD_DIM
    q_spec = pl.BlockSpec((1, t, gw), lambda bi, hi, qi: (bi, qi, base + hi))
    k_spec = pl.BlockSpec((1, s, gw), lambda bi, hi, qi: (bi, 0, base + ng + hi))
    v_spec = pl.BlockSpec((1, s, gw), lambda bi, hi, qi: (bi, 0, base + 2 * ng + hi))
    o_spec = pl.BlockSpec((1, t, gw), lambda bi, hi, qi: (bi, qi, hi))
    out_shape = jax.ShapeDtypeStruct((b, s, h * HEAD_DIM), BF16)
    grid = (b, ng, s // t)
    sem = ("parallel", "parallel", "arbitrary")
    if which == "sb":
        return pl.pallas_call(
            functools.partial(_sb_attn_kernel, tile=t, scale=scale, group=grp),
            out_shape=out_shape, grid=grid,
            in_specs=[q_spec, k_spec, v_spec], out_specs=o_spec,
            scratch_shapes=[pltpu.VMEM((t, t), BF16),
                            pltpu.VMEM((t, gw), F32),
                            pltpu.VMEM((t, grp * LANES), F32),
                            pltpu.VMEM((t, grp * t), F32)],
            compiler_params=_params(sem), name="sb_attention",
        )(qkv, qkv, qkv)
    c_spec = pl.BlockSpec((1, grp, 1, s), lambda bi, hi, qi: (bi, hi, 0, 0))
    return pl.pallas_call(
        functools.partial(_fx_attn_kernel, tile=t, scale=scale, group=grp),
        out_shape=out_shape, grid=grid,
        in_specs=[q_spec, k_spec, v_spec, c_spec], out_specs=o_spec,
        scratch_shapes=[pltpu.VMEM((t, grp * LANES), F32),
                        pltpu.VMEM((t, grp * 2 * LANES), F32),
                        pltpu.VMEM((t, grp * t), F32)],
        compiler_params=_params(sem), name="fx_attention",
    )(qkv, qkv, qkv, c_rows)


def _layer_norm(y, g, b):
    mu = jnp.mean(y, axis=-1, keepdims=True)
    d = y - mu
    var = jnp.mean(d * d, axis=-1, keepdims=True)
    return d * lax.rsqrt(var + LN_EPS) * g + b


def _post_attn_kernel(osb_ref, ofx_ref, gsb_ref, gfx_ref, wsb_ref, wfx_ref, wout_ref,
                      x_hbm, g_ref, b_ref, h_ref, x_buf, sem, *, alpha, tm):
    i = pl.program_id(0)
    c = pl.program_id(1)
    x_copy = pltpu.make_async_copy(x_hbm.at[pl.ds(i * tm, tm), :], x_buf, sem)

    @pl.when(c == 0)
    def _():
        x_copy.start()
        h_ref[...] = jnp.zeros_like(h_ref)

    tsb = jnp.dot(osb_ref[...], wsb_ref[...], preferred_element_type=F32)
    tfx = jnp.dot(ofx_ref[...], wfx_ref[...], preferred_element_type=F32)
    merged = gsb_ref[...].astype(F32) * tsb + gfx_ref[...].astype(F32) * tfx
    h_ref[...] += jnp.dot(merged.astype(BF16), wout_ref[...], preferred_element_type=F32)

    @pl.when(c == pl.num_programs(1) - 1)
    def _():
        x_copy.wait()
        h_ref[...] = _layer_norm(alpha * x_buf[...] + h_ref[...], g_ref[...], b_ref[...])


def _post_attention(o_sb, o_fx, gates, w_sb, w_fx, w_out, x, ln_g, ln_b, cfg, alpha):
    m, d = x.shape
    width = o_sb.shape[1]
    tm, tc = min(cfg.post_tm, m), min(cfg.post_tc, d)
    nc = d // tc
    return pl.pallas_call(
        functools.partial(_post_attn_kernel, alpha=alpha, tm=tm),
        out_shape=jax.ShapeDtypeStruct((m, d), F32),
        grid=(m // tm, nc),
        in_specs=[pl.BlockSpec((tm, width), lambda i, c: (i, 0)),
                  pl.BlockSpec((tm, width), lambda i, c: (i, 0)),
                  pl.BlockSpec((tm, tc), lambda i, c: (i, c)),
                  pl.BlockSpec((tm, tc), lambda i, c: (i, nc + c)),
                  pl.BlockSpec((width, tc), lambda i, c: (0, c)),
                  pl.BlockSpec((width, tc), lambda i, c: (0, c)),
                  pl.BlockSpec((tc, d), lambda i, c: (c, 0)),
                  pl.BlockSpec(memory_space=pl.ANY),
                  pl.BlockSpec((1, d), lambda i, c: (0, 0)),
                  pl.BlockSpec((1, d), lambda i, c: (0, 0))],
        out_specs=pl.BlockSpec((tm, d), lambda i, c: (i, 0)),
        scratch_shapes=[pltpu.VMEM((tm, d), F32), pltpu.SemaphoreType.DMA],
        compiler_params=_params(("arbitrary", "arbitrary")),
        name="post_attention",
    )(o_sb, o_fx, gates, gates, w_sb, w_fx, w_out, x, ln_g, ln_b)


def _router_kernel(h_ref, w_ref, b_ref, idx_ref, gate_ref):
    logits = jnp.dot(h_ref[...], w_ref[...], preferred_element_type=F32,
                     precision=lax.Precision.HIGHEST) + b_ref[...]
    lane = lax.broadcasted_iota(jnp.int32, logits.shape, 1)
    lane_f = lane.astype(F32)
    vals = logits
    idx_out = jnp.zeros(logits.shape, F32)
    exp_out = jnp.zeros(logits.shape, F32)
    denom = jnp.zeros((logits.shape[0], 1), F32)
    top = None
    for k in range(TOP_K):
        m = jnp.max(vals, axis=1, keepdims=True)
        sel = jnp.min(jnp.where(vals == m, lane_f, float(LANES)), axis=1, keepdims=True)
        if k == 0:
            top = m
        e = jnp.exp(m - top)
        denom = denom + e
        idx_out = jnp.where(lane == k, sel, idx_out)
        exp_out = jnp.where(lane == k, e, exp_out)
        vals = jnp.where(lane_f == sel, MASK_VALUE * 2.0, vals)
    idx_ref[...] = idx_out.astype(jnp.int32)
    gate_ref[...] = exp_out / denom


def _router(h, w_router, b_router, cfg):
    m, d = h.shape
    e = cfg.n_experts
    tm = min(cfg.route_tm, m)
    w_pad = jnp.zeros((d, LANES), F32).at[:, :e].set(w_router)
    b_pad = jnp.full((1, LANES), MASK_VALUE, F32).at[0, :e].set(b_router)
    return pl.pallas_call(
        _router_kernel,
        out_shape=(jax.ShapeDtypeStruct((m, LANES), jnp.int32),
                   jax.ShapeDtypeStruct((m, LANES), F32)),
        grid=(m // tm,),
        in_specs=[pl.BlockSpec((tm, d), lambda i: (i, 0)),
                  pl.BlockSpec((d, LANES), lambda i: (0, 0)),
                  pl.BlockSpec((1, LANES), lambda i: (0, 0))],
        out_specs=(pl.BlockSpec((tm, LANES), lambda i: (i, 0)),
                   pl.BlockSpec((tm, LANES), lambda i: (i, 0))),
        compiler_params=_params(("parallel",)),
        name="router",
    )(h, w_pad, b_pad)


def _row_copy(src_hbm, row, dst_vmem, slot, sem):
    return pltpu.make_async_copy(src_hbm.at[pl.ds(row, 1), :], dst_vmem.at[pl.ds(slot, 1), :], sem)


def _slab_pitch(n_slab):
    return n_slab + 4 if n_slab % 8 == 0 else n_slab


def _slab_copy(src_hbm, row, dst_vmem, slot, sem):
    n_slab = src_hbm.shape[1]
    dst = dst_vmem.at[pl.ds(slot * _slab_pitch(n_slab), n_slab), :]
    return pltpu.make_async_copy(src_hbm.at[row], dst, sem)


def _row_slabs(x):
    return x.reshape(x.shape[0], x.shape[1] // LANES, LANES)


def _gather_rows_kernel(tok_ref, src_ref, nvalid_ref, h_hbm, o_ref, buf_ref, sem, *, tm):
    blk = pl.program_id(0)

    def copy(b, r):
        slot = b % 2
        return _slab_copy(h_hbm, tok_ref[src_ref[b] + r], buf_ref.at[slot], r, sem.at[slot])

    def issue(b):
        def body(r, c):
            copy(b, r).start()
            return c
        lax.fori_loop(0, nvalid_ref[b], body, 0)

    @pl.when(blk == 0)
    def _():
        buf_ref[...] = jnp.zeros_like(buf_ref)
        issue(0)

    @pl.when(blk + 1 < pl.num_programs(0))
    def _():
        issue(blk + 1)

    def drain(r, c):
        copy(blk, r).wait()
        return c

    lax.fori_loop(0, nvalid_ref[blk], drain, 0)
    slot = blk % 2
    valid = lax.broadcasted_iota(jnp.int32, (tm, 1), 0) < nvalid_ref[blk]
    n_slab = h_hbm.shape[1]
    for s in range(n_slab):
        chunk = jnp.where(valid, buf_ref[slot, pl.ds(s, tm, stride=_slab_pitch(n_slab)), :], 0.0)
        o_ref[:, s * LANES:(s + 1) * LANES] = chunk.astype(o_ref.dtype)


def _gather_rows(h_slabs, sorted_tok, blk_src, blk_nvalid, cfg):
    _, n_slab, _ = h_slabs.shape
    tm = cfg.moe_tm
    n_blk = blk_src.shape[0]
    return pl.pallas_call(
        functools.partial(_gather_rows_kernel, tm=tm),
        out_shape=jax.ShapeDtypeStruct((n_blk * tm, n_slab * LANES), BF16),
        grid_spec=pltpu.PrefetchScalarGridSpec(
            num_scalar_prefetch=3, grid=(n_blk,),
            in_specs=[pl.BlockSpec(memory_space=pl.ANY)],
            out_specs=pl.BlockSpec((tm, n_slab * LANES), lambda i, tok, src, nv: (i, 0)),
            scratch_shapes=[pltpu.VMEM((2, tm * _slab_pitch(n_slab), LANES), F32),
                            pltpu.SemaphoreType.DMA((2,))]),
        compiler_params=_params(("arbitrary",)),
        name="moe_gather",
    )(sorted_tok, blk_src, blk_nvalid, h_slabs)


def _expert_up_kernel(blk_e_ref, nused_ref, x_ref, wg_ref, wu_ref, bg_ref, bu_ref, act_ref,
                      wg_bf, wu_bf):
    blk = pl.program_id(1)
    e = blk_e_ref[blk]
    e_prev = blk_e_ref[jnp.maximum(blk - 1, 0)]

    @pl.when((blk == 0) | (e != e_prev))
    def _():
        wg_bf[...] = wg_ref[0].astype(BF16)
        wu_bf[...] = wu_ref[0].astype(BF16)

    @pl.when(blk < nused_ref[0])
    def _():
        x = x_ref[...]
        g = jnp.dot(x, wg_bf[...], preferred_element_type=F32) + bg_ref[0]
        u = jnp.dot(x, wu_bf[...], preferred_element_type=F32) + bu_ref[0]
        g = jnp.minimum(g, SWIGLU_LIMIT)
        u = jnp.clip(u, -SWIGLU_LIMIT, SWIGLU_LIMIT)
        act_ref[...] = (g * jax.nn.sigmoid(SWIGLU_ALPHA * g) * (u + 1.0)).astype(act_ref.dtype)

    @pl.when(blk >= nused_ref[0])
    def _():
        act_ref[...] = jnp.zeros_like(act_ref)


def _expert_up(rows, w_up, b_up, blk_e, n_used, cfg):
    n_rows, d = rows.shape
    d_ff = w_up.shape[2] // 2
    tm, tf = cfg.moe_tm, min(cfg.moe_tf, d_ff)
    nf = d_ff // tf
    n_blk = n_rows // tm
    b_up3 = b_up.reshape(b_up.shape[0], 1, 2 * d_ff)
    return pl.pallas_call(
        _expert_up_kernel,
        out_shape=jax.ShapeDtypeStruct((n_rows, d_ff), BF16),
        grid_spec=pltpu.PrefetchScalarGridSpec(
            num_scalar_prefetch=2, grid=(nf, n_blk),
            in_specs=[pl.BlockSpec((tm, d), lambda f, i, be, nu: (i, 0)),
                      pl.BlockSpec((1, d, tf), lambda f, i, be, nu: (be[i], 0, f)),
                      pl.BlockSpec((1, d, tf), lambda f, i, be, nu: (be[i], 0, nf + f)),
                      pl.BlockSpec((1, 1, tf), lambda f, i, be, nu: (be[i], 0, f)),
                      pl.BlockSpec((1, 1, tf), lambda f, i, be, nu: (be[i], 0, nf + f))],
            out_specs=pl.BlockSpec((tm, tf), lambda f, i, be, nu: (i, f)),
            scratch_shapes=[pltpu.VMEM((d, tf), BF16), pltpu.VMEM((d, tf), BF16)]),
        compiler_params=_params(("arbitrary", "arbitrary")),
        name="moe_up",
    )(blk_e, n_used, rows, w_up, w_up, b_up3, b_up3)


def _expert_down_kernel(blk_e_ref, nused_ref, a_ref, w_ref, b_ref, y_ref, w_bf):
    blk = pl.program_id(1)
    e = blk_e_ref[blk]
    e_prev = blk_e_ref[jnp.maximum(blk - 1, 0)]

    @pl.when((blk == 0) | (e != e_prev))
    def _():
        w_bf[...] = w_ref[0].astype(BF16)

    @pl.when(blk < nused_ref[0])
    def _():
        y_ref[...] = jnp.dot(a_ref[...], w_bf[...], preferred_element_type=F32) + b_ref[0]

    @pl.when(blk >= nused_ref[0])
    def _():
        y_ref[...] = jnp.zeros_like(y_ref)


def _expert_down(act, w_down, b_down, blk_e, n_used, cfg):
    n_rows, d_ff = act.shape
    d = w_down.shape[2]
    tm, tn = cfg.moe_tm, min(cfg.moe_tn, d)
    n_blk = n_rows // tm
    b_down3 = b_down.reshape(b_down.shape[0], 1, d)
    return pl.pallas_call(
        _expert_down_kernel,
        out_shape=jax.ShapeDtypeStruct((n_rows, d), F32),
        grid_spec=pltpu.PrefetchScalarGridSpec(
            num_scalar_prefetch=2, grid=(d // tn, n_blk),
            in_specs=[pl.BlockSpec((tm, d_ff), lambda n, i, be, nu: (i, 0)),
                      pl.BlockSpec((1, d_ff, tn), lambda n, i, be, nu: (be[i], 0, n)),
                      pl.BlockSpec((1, 1, tn), lambda n, i, be, nu: (be[i], 0, n))],
            out_specs=pl.BlockSpec((tm, tn), lambda n, i, be, nu: (i, n)),
            scratch_shapes=[pltpu.VMEM((d_ff, tn), BF16)]),
        compiler_params=_params(("arbitrary", "arbitrary")),
        name="moe_down",
    )(blk_e, n_used, act, w_down, b_down3)


def _combine_kernel(dest_ref, y_hbm, gate_ref, h_ref, g_ref, b_ref, o_ref, buf_ref, sem,
                    *, tt, alpha):
    i = pl.program_id(0)
    n_copy = tt * TOP_K

    def copy(tile, a):
        slot = tile % 2
        return _row_copy(y_hbm, dest_ref[tile * n_copy + a], buf_ref.at[slot], a, sem.at[slot])

    def issue(tile):
        def body(a, c):
            copy(tile, a).start()
            return c
        lax.fori_loop(0, n_copy, body, 0)

    @pl.when(i == 0)
    def _():
        issue(0)

    @pl.when(i + 1 < pl.num_programs(0))
    def _():
        issue(i + 1)

    def drain(a, c):
        copy(i, a).wait()
        return c

    lax.fori_loop(0, n_copy, drain, 0)
    slot = i % 2
    gate = gate_ref[...]
    ffn = gate[:, 0:1] * buf_ref[slot, pl.ds(0, tt), :]
    for k in range(1, TOP_K):
        ffn = ffn + gate[:, k:k + 1] * buf_ref[slot, pl.ds(k * tt, tt), :]
    o_ref[...] = _layer_norm(alpha * h_ref[...] + ffn, g_ref[...], b_ref[...])


def _combine(y_slabs, dest_kmajor, gate, h, ln_g, ln_b, cfg, alpha):
    m, d = h.shape
    tt = min(cfg.comb_tt, m)
    return pl.pallas_call(
        functools.partial(_combine_kernel, tt=tt, alpha=alpha),
        out_shape=jax.ShapeDtypeStruct((m, d), F32),
        grid_spec=pltpu.PrefetchScalarGridSpec(
            num_scalar_prefetch=1, grid=(m // tt,),
            in_specs=[pl.BlockSpec(memory_space=pl.ANY),
                      pl.BlockSpec((tt, LANES), lambda i, dst: (i, 0)),
                      pl.BlockSpec((tt, d), lambda i, dst: (i, 0)),
                      pl.BlockSpec((1, d), lambda i, dst: (0, 0)),
                      pl.BlockSpec((1, d), lambda i, dst: (0, 0))],
            out_specs=pl.BlockSpec((tt, d), lambda i, dst: (i, 0)),
            scratch_shapes=[pltpu.VMEM((2, tt * TOP_K, d), F32), pltpu.SemaphoreType.DMA((2,))]),
        compiler_params=_params(("arbitrary",)),
        name="moe_combine",
    )(dest_kmajor, y_slabs, gate, h, ln_g, ln_b)


def _routing_tables(top_idx, cfg):
    n_tok = top_idx.shape[0]
    e, tm = cfg.n_experts, cfg.moe_tm
    n_assign = n_tok * TOP_K
    n_blk = -(-n_assign // tm) + e
    flat_e = top_idx.reshape(-1)
    assign = jnp.arange(n_assign, dtype=jnp.int32)
    sorted_e, sorted_a = lax.sort((flat_e, assign), num_keys=2)
    experts = jnp.arange(e, dtype=jnp.int32)
    onehot = sorted_e[:, None] == experts[None, :]
    counts = jnp.sum(onehot, axis=0, dtype=jnp.int32)
    start = jnp.cumsum(counts) - counts
    blocks_e = (counts + tm - 1) // tm
    blk_end = jnp.cumsum(blocks_e)
    blk_start = blk_end - blocks_e
    shift = blk_start * tm - start
    dest_sorted = assign + jnp.sum(jnp.where(onehot, shift[None, :], 0), axis=1, dtype=jnp.int32)
    _, dest = lax.sort((sorted_a, dest_sorted), num_keys=1)
    blk = jnp.arange(n_blk, dtype=jnp.int32)
    blk_onehot = (blk[:, None] >= blk_start[None, :]) & (blk[:, None] < blk_end[None, :])
    blk_e = jnp.sum(jnp.where(blk_onehot, experts[None, :], 0), axis=1, dtype=jnp.int32)
    n_used = blk_end[-1:].astype(jnp.int32)
    blk_e = jnp.where(blk < n_used[0], blk_e, e - 1)
    local = (blk - jnp.sum(jnp.where(blk_onehot, blk_start[None, :], 0), axis=1)) * tm
    blk_count = jnp.sum(jnp.where(blk_onehot, counts[None, :], 0), axis=1)
    blk_src = (jnp.sum(jnp.where(blk_onehot, start[None, :], 0), axis=1) + local).astype(jnp.int32)
    blk_nvalid = jnp.clip(blk_count - local, 0, tm).astype(jnp.int32)
    sorted_tok = sorted_a // TOP_K
    return dest.astype(jnp.int32), blk_e, n_used, sorted_tok, blk_src, blk_nvalid


def _moe(h, w_router, b_router, w_up, b_up, w_down, b_down, ln_g, ln_b, cfg, alpha):
    n_tok = h.shape[0]
    idx_pad, gate_pad = _router(h, w_router, b_router, cfg)
    dest, blk_e, n_used, sorted_tok, blk_src, blk_nvalid = _routing_tables(idx_pad[:, :TOP_K], cfg)
    rows = _gather_rows(_row_slabs(h), sorted_tok, blk_src, blk_nvalid, cfg)
    act = _expert_up(rows, w_up, b_up, blk_e, n_used, cfg)
    y_slabs = _expert_down(act, w_down, b_down, blk_e, n_used, cfg)
    tt = min(cfg.comb_tt, n_tok)
    dest_kmajor = dest.reshape(n_tok // tt, tt, TOP_K).transpose(0, 2, 1).reshape(-1)
    return _combine(y_slabs, dest_kmajor, gate_pad, h, ln_g, ln_b, cfg, alpha)


def _layer(h, w_in, b_in, w_sb, w_fx, w_out, ln1_g, ln1_b, w_router, b_router,
           w_up, b_up, w_down, b_down, ln2_g, ln2_b, cfg):
    b, s, d = h.shape
    heads = cfg.heads
    width = heads * HEAD_DIM
    alpha = (2.0 * cfg.depth) ** 0.25
    n_tok = b * s
    x = h.reshape(n_tok, d)
    x_bf = x.astype(BF16)
    o_f = 6 * width
    o_g = o_f + heads
    qkv = _matmul(x_bf, w_in[:, :o_f].astype(BF16), b_in[None, :o_f], out_dtype=BF16, act=None,
                  tm=cfg.mm_tm, tn=cfg.mm_tn, name="in_proj_qkv")
    gates = _matmul(x_bf, w_in[:, o_g:].astype(BF16), b_in[None, o_g:], out_dtype=BF16,
                    act="sigmoid", tm=cfg.mm_tm, tn=cfg.mm_tn, name="in_proj_gates")
    w_f = jnp.zeros((d, LANES), BF16).at[:, :heads].set(w_in[:, o_f:o_g].astype(BF16))
    b_f = jnp.zeros((1, LANES), F32).at[0, :heads].set(b_in[o_f:o_g])
    f_logit = _matmul(x_bf, w_f, b_f, out_dtype=F32, act=None,
                      tm=cfg.mm_tm, tn=LANES, name="in_proj_forget")
    f_rows = f_logit[:, :heads].reshape(b, s, heads).transpose(0, 2, 1).reshape(b * heads, s)
    c_rows = _forget_cumsum(f_rows).reshape(b, heads, 1, s)

    qkv3 = qkv.reshape(b, s, o_f)
    o_sb = _attention(qkv3, None, cfg, "sb").reshape(n_tok, width)
    o_fx = _attention(qkv3, c_rows, cfg, "fx").reshape(n_tok, width)

    h1 = _post_attention(o_sb, o_fx, gates, w_sb.astype(BF16), w_fx.astype(BF16),
                         w_out.astype(BF16), x, ln1_g[None, :], ln1_b[None, :], cfg, alpha)
    out = _moe(h1, w_router, b_router, w_up, b_up, w_down, b_down,
               ln2_g[None, :], ln2_b[None, :], cfg, alpha)
    return out.reshape(b, s, d)


def kernel(x, w_in, b_in, w_branch_sb, w_branch_fx, w_out, ln1_g, ln1_b, w_router, b_router,
           w_up, b_up, w_down, b_down, ln2_g, ln2_b):
    depth = w_in.shape[0]
    d = x.shape[-1]
    n_experts = w_router.shape[-1]
    heads = (w_in.shape[-1] - 2 * d) // (6 * HEAD_DIM + 1)
    cfg = Config(heads=heads, n_experts=n_experts, depth=depth,
                 mm_tm=1024, mm_tn=1024, attn_tile=256, attn_group=4, post_tm=512, post_tc=512,
                 route_tm=512, moe_tm=512, moe_tf=512, moe_tn=2048, comb_tt=128)
    h = x
    for l in range(depth):
        h = _layer(h, w_in[l], b_in[l], w_branch_sb[l], w_branch_fx[l], w_out[l],
                   ln1_g[l], ln1_b[l], w_router[l], b_router[l], w_up[l], b_up[l],
                   w_down[l], b_down[l], ln2_g[l], ln2_b[l], cfg)
    return h
```

```python
import functools
from typing import NamedTuple

import jax
import jax.numpy as jnp
from jax import lax
from jax.experimental import pallas as pl
from jax.experimental.pallas import tpu as pltpu

F32 = jnp.float32
BF16 = jnp.bfloat16

LANES = 128
HEAD_DIM = 128
TOP_K = 4
SWIGLU_ALPHA = 1.702
SWIGLU_LIMIT = 7.0
LN_EPS = 1e-5
LOG2_E = 1.4426950408889634
MASK_VALUE = -1e30
SB_EXIT_LOG2 = -160.0
ROW_UNROLL = 8
VMEM_LIMIT_BYTES = 60 * 1024 * 1024


class Config(NamedTuple):
    heads: int
    n_experts: int
    depth: int
    mm_tm: int
    mm_tn: int
    attn_tile: int
    attn_group: int
    post_tm: int
    post_tc: int
    route_tm: int
    moe_tm: int
    moe_tf: int
    moe_tn: int
    comb_tt: int


def _params(sem):
    return pltpu.CompilerParams(dimension_semantics=sem, vmem_limit_bytes=VMEM_LIMIT_BYTES)


def _matmul_kernel(x_ref, w_ref, b_ref, o_ref, *, act):
    acc = jnp.dot(x_ref[...], w_ref[...], preferred_element_type=F32) + b_ref[...]
    if act == "sigmoid":
        acc = jax.nn.sigmoid(acc)
    o_ref[...] = acc.astype(o_ref.dtype)


def _matmul(x, w, b, *, out_dtype, act, tm, tn, name):
    m, k = x.shape
    n = w.shape[1]
    tm, tn = min(tm, m), min(tn, n)
    return pl.pallas_call(
        functools.partial(_matmul_kernel, act=act),
        out_shape=jax.ShapeDtypeStruct((m, n), out_dtype),
        grid=(m // tm, n // tn),
        in_specs=[pl.BlockSpec((tm, k), lambda i, j: (i, 0)),
                  pl.BlockSpec((k, tn), lambda i, j: (0, j)),
                  pl.BlockSpec((1, tn), lambda i, j: (0, j))],
        out_specs=pl.BlockSpec((tm, tn), lambda i, j: (i, j)),
        compiler_params=_params(("parallel", "arbitrary")),
        name=name,
    )(x, w, b)


def _forget_cumsum_kernel(f_ref, c_ref):
    f = f_ref[...]
    x = jnp.minimum(f, 0.0) - jnp.log(1.0 + jnp.exp(-jnp.abs(f)))
    n = x.shape[1]
    lane = lax.broadcasted_iota(jnp.int32, x.shape, 1)
    d = 1
    while d < n:
        x = x + jnp.where(lane >= d, pltpu.roll(x, d, axis=1), 0.0)
        d *= 2
    c_ref[...] = x


def _forget_cumsum(f_logit_rows):
    return pl.pallas_call(
        _forget_cumsum_kernel,
        out_shape=jax.ShapeDtypeStruct(f_logit_rows.shape, F32),
        name="forget_cumsum",
    )(f_logit_rows)


def _qk(q, k):
    return lax.dot_general(q, k, (((1,), (1,)), ((), ())), preferred_element_type=F32)


def _lane_repeat(x, width):
    return jnp.concatenate([x] * (width // LANES), axis=1)


def _neg_abs(x):
    bits = pltpu.bitcast(x, jnp.uint32) | jnp.uint32(0x80000000)
    return pltpu.bitcast(bits, F32)


def _sb_attn_kernel(q_ref, k_ref, v_ref, o_ref, u_ref, acc_ref, carry_ref, z_ref, *, tile, scale,
                    group):
    i = pl.program_id(2)
    t = tile
    q = q_ref[0]
    row = lax.broadcasted_iota(jnp.int32, (t, t), 0)
    col = lax.broadcasted_iota(jnp.int32, (t, t), 1)
    strict = col < row
    u_ref[...] = (row > col).astype(BF16)
    acc_ref[...] = jnp.zeros_like(acc_ref)
    carry_ref[...] = jnp.zeros_like(carry_ref)

    heads = [slice(g * HEAD_DIM, (g + 1) * HEAD_DIM) for g in range(group)]

    def scores(j):
        kj = k_ref[0, pl.ds(pl.multiple_of(j * t, t), t), :]
        return [_qk(q[:, hd], kj[:, hd]) for hd in heads]

    def step(j, masked):
        z_next = scores(jnp.maximum(j - 1, 0))
        if masked:
            z_raw = scores(j)
        else:
            z_raw = [z_ref[:, g * t:(g + 1) * t] for g in range(group)]
        vj = v_ref[0, pl.ds(pl.multiple_of(j * t, t), t), :]
        u = u_ref[...]
        carry = carry_ref[...]
        z = [zg * (scale * LOG2_E) for zg in z_raw]
        log_beta, log_keep, row_sum = [], [], []
        for zg in z:
            lb = jnp.minimum(zg, 0.0) - jnp.log2(1.0 + jnp.exp2(_neg_abs(zg)))
            lk = lb - zg
            if masked:
                lk = jnp.where(strict, lk, 0.0)
            log_beta.append(lb)
            log_keep.append(lk.astype(BF16))
            row_sum.append(jnp.broadcast_to(jnp.sum(lk, axis=1, keepdims=True), (t, LANES)))
        after = [jnp.dot(lk, u, preferred_element_type=F32) for lk in log_keep]
        w = []
        for hd, lb, af in zip(heads, log_beta, after):
            wg = jnp.exp2(lb + af + _lane_repeat(carry[:, hd], t))
            if masked:
                wg = jnp.where(strict, wg, 0.0)
            w.append(wg.astype(BF16))
        pv = [jnp.dot(wg, vj[:, hd], preferred_element_type=F32) for hd, wg in zip(heads, w)]
        acc_ref[...] += jnp.concatenate(pv, axis=1)
        carry_ref[...] = carry + jnp.concatenate(row_sum, axis=1)
        z_ref[...] = jnp.concatenate(z_next, axis=1)

    step(i, True)

    def alive():
        return jnp.max(carry_ref[...]) > SB_EXIT_LOG2

    def cond(state):
        it, go = state
        return (it < i) & go

    def body(state):
        it, _ = state
        step(i - 1 - it, False)
        return it + 1, alive()

    lax.while_loop(cond, body, (jnp.int32(0), alive()))
    o_ref[0] = acc_ref[...].astype(o_ref.dtype)


def _fx_attn_kernel(q_ref, k_ref, v_ref, c_ref, o_ref, m_ref, acc_ref, z_ref, *, tile, scale,
                    group):
    i = pl.program_id(2)
    t = tile
    q = q_ref[0]
    row = lax.broadcasted_iota(jnp.int32, (t, t), 0)
    col = lax.broadcasted_iota(jnp.int32, (t, t), 1)
    causal = col <= row
    m_ref[...] = jnp.full_like(m_ref, MASK_VALUE)
    acc_ref[...] = jnp.zeros_like(acc_ref)
    ones = jnp.ones((t, LANES), BF16)
    heads = [slice(g * HEAD_DIM, (g + 1) * HEAD_DIM) for g in range(group)]

    def scores(j):
        kj = k_ref[0, pl.ds(pl.multiple_of(j * t, t), t), :]
        return jnp.concatenate([_qk(q[:, hd], kj[:, hd]) for hd in heads], axis=1)

    def step(j, masked):
        if not masked:
            z_next = scores(j + 1)
        start = pl.multiple_of(j * t, t)
        vj = v_ref[0, pl.ds(start, t), :]
        m_all = m_ref[...]
        p, rescale, m_new = [], [], []
        for g, hd in enumerate(heads):
            s = (z_ref[:, g * t:(g + 1) * t] * (scale * LOG2_E)
                 - c_ref[0, g, :, pl.ds(start, t)] * LOG2_E)
            if masked:
                s = jnp.where(causal, s, MASK_VALUE)
            m_prev = m_all[:, hd]
            m_next = jnp.maximum(m_prev, jnp.max(s, axis=1, keepdims=True))
            p.append(jnp.exp2(s - _lane_repeat(m_next, t)).astype(BF16))
            rescale.append(_lane_repeat(jnp.exp2(m_prev - m_next), 2 * LANES))
            m_new.append(m_next)
        pv = [jnp.dot(pg, jnp.concatenate([vj[:, hd], ones], axis=1), preferred_element_type=F32)
              for hd, pg in zip(heads, p)]
        acc_ref[...] = jnp.concatenate(rescale, axis=1) * acc_ref[...] + jnp.concatenate(pv, axis=1)
        m_ref[...] = jnp.concatenate(m_new, axis=1)
        if not masked:
            z_ref[...] = z_next

    def body(j, c):
        step(j, False)
        return c

    z_ref[...] = scores(0)
    lax.fori_loop(0, i, body, 0)
    step(i, True)
    for g in range(group):
        acc = acc_ref[:, 2 * g * LANES:2 * (g + 1) * LANES]
        o_ref[0, :, g * HEAD_DIM:(g + 1) * HEAD_DIM] = (
            acc[:, :LANES] / acc[:, LANES:]).astype(o_ref.dtype)


def _attention(qkv, c_rows, cfg, which):
    b, s, _ = qkv.shape
    h = cfg.heads
    grp = cfg.attn_group
    ng = h // grp
    t = min(cfg.attn_tile, s)
    base = 0 if which == "sb" else 3 * ng
    scale = HEAD_DIM ** -0.5
    gw = grp * HEAD_DIM
    q_spec = pl.BlockSpec((1, t, gw), lambda bi, hi, qi: (bi, qi, base + hi))
    k_spec = pl.BlockSpec((1, s, gw), lambda bi, hi, qi: (bi, 0, base + ng + hi))
    v_spec = pl.BlockSpec((1, s, gw), lambda bi, hi, qi: (bi, 0, base + 2 * ng + hi))
    o_spec = pl.BlockSpec((1, t, gw), lambda bi, hi, qi: (bi, qi, hi))
    out_shape = jax.ShapeDtypeStruct((b, s, h * HEAD_DIM), BF16)
    grid = (b, ng, s // t)
    sem = ("parallel", "parallel", "arbitrary")
    if which == "sb":
        return pl.pallas_call(
            functools.partial(_sb_attn_kernel, tile=t, scale=scale, group=grp),
            out_shape=out_shape, grid=grid,
            in_specs=[q_spec, k_spec, v_spec], out_specs=o_spec,
            scratch_shapes=[pltpu.VMEM((t, t), BF16),
                            pltpu.VMEM((t, gw), F32),
                            pltpu.VMEM((t, grp * LANES), F32),
                            pltpu.VMEM((t, grp * t), F32)],
            compiler_params=_params(sem), name="sb_attention",
        )(qkv, qkv, qkv)
    c_spec = pl.BlockSpec((1, grp, 1, s), lambda bi, hi, qi: (bi, hi, 0, 0))
    return pl.pallas_call(
        functools.partial(_fx_attn_kernel, tile=t, scale=scale, group=grp),
        out_shape=out_shape, grid=grid,
        in_specs=[q_spec, k_spec, v_spec, c_spec], out_specs=o_spec,
        scratch_shapes=[pltpu.VMEM((t, grp * LANES), F32),
                        pltpu.VMEM((t, grp * 2 * LANES), F32),
                        pltpu.VMEM((t, grp * t), F32)],
        compiler_params=_params(sem), name="fx_attention",
    )(qkv, qkv, qkv, c_rows)


def _layer_norm(y, g, b):
    mu = jnp.mean(y, axis=-1, keepdims=True)
    d = y - mu
    var = jnp.mean(d * d, axis=-1, keepdims=True)
    return d * lax.rsqrt(var + LN_EPS) * g + b


def _post_attn_kernel(osb_ref, ofx_ref, gsb_ref, gfx_ref, wsb_ref, wfx_ref, wout_ref,
                      x_hbm, g_ref, b_ref, h_ref, x_buf, sem, *, alpha, tm):
    i = pl.program_id(0)
    c = pl.program_id(1)
    x_copy = pltpu.make_async_copy(x_hbm.at[pl.ds(i * tm, tm), :], x_buf, sem)

    @pl.when(c == 0)
    def _():
        x_copy.start()
        h_ref[...] = jnp.zeros_like(h_ref)

    tsb = jnp.dot(osb_ref[...], wsb_ref[...], preferred_element_type=F32)
    tfx = jnp.dot(ofx_ref[...], wfx_ref[...], preferred_element_type=F32)
    merged = gsb_ref[...].astype(F32) * tsb + gfx_ref[...].astype(F32) * tfx
    h_ref[...] += jnp.dot(merged.astype(BF16), wout_ref[...], preferred_element_type=F32)

    @pl.when(c == pl.num_programs(1) - 1)
    def _():
        x_copy.wait()
        h_ref[...] = _layer_norm(alpha * x_buf[...] + h_ref[...], g_ref[...], b_ref[...])


def _post_attention(o_sb, o_fx, gates, w_sb, w_fx, w_out, x, ln_g, ln_b, cfg, alpha):
    m, d = x.shape
    width = o_sb.shape[1]
    tm, tc = min(cfg.post_tm, m), min(cfg.post_tc, d)
    nc = d // tc
    return pl.pallas_call(
        functools.partial(_post_attn_kernel, alpha=alpha, tm=tm),
        out_shape=jax.ShapeDtypeStruct((m, d), F32),
        grid=(m // tm, nc),
        in_specs=[pl.BlockSpec((tm, width), lambda i, c: (i, 0)),
                  pl.BlockSpec((tm, width), lambda i, c: (i, 0)),
                  pl.BlockSpec((tm, tc), lambda i, c: (i, c)),
                  pl.BlockSpec((tm, tc), lambda i, c: (i, nc + c)),
                  pl.BlockSpec((width, tc), lambda i, c: (0, c)),
                  pl.BlockSpec((width, tc), lambda i, c: (0, c)),
                  pl.BlockSpec((tc, d), lambda i, c: (c, 0)),
                  pl.BlockSpec(memory_space=pl.ANY),
                  pl.BlockSpec((1, d), lambda i, c: (0, 0)),
                  pl.BlockSpec((1, d), lambda i, c: (0, 0))],
        out_specs=pl.BlockSpec((tm, d), lambda i, c: (i, 0)),
        scratch_shapes=[pltpu.VMEM((tm, d), F32), pltpu.SemaphoreType.DMA],
        compiler_params=_params(("arbitrary", "arbitrary")),
        name="post_attention",
    )(o_sb, o_fx, gates, gates, w_sb, w_fx, w_out, x, ln_g, ln_b)


def _router_kernel(h_ref, w_ref, b_ref, idx_ref, gate_ref):
    logits = jnp.dot(h_ref[...], w_ref[...], preferred_element_type=F32,
                     precision=lax.Precision.HIGHEST) + b_ref[...]
    lane = lax.broadcasted_iota(jnp.int32, logits.shape, 1)
    lane_f = lane.astype(F32)
    vals = logits
    idx_out = jnp.zeros(logits.shape, F32)
    exp_out = jnp.zeros(logits.shape, F32)
    denom = jnp.zeros((logits.shape[0], 1), F32)
    top = None
    for k in range(TOP_K):
        m = jnp.max(vals, axis=1, keepdims=True)
        sel = jnp.min(jnp.where(vals == m, lane_f, float(LANES)), axis=1, keepdims=True)
        if k == 0:
            top = m
        e = jnp.exp(m - top)
        denom = denom + e
        idx_out = jnp.where(lane == k, sel, idx_out)
        exp_out = jnp.where(lane == k, e, exp_out)
        vals = jnp.where(lane_f == sel, MASK_VALUE * 2.0, vals)
    idx_ref[...] = idx_out.astype(jnp.int32)
    gate_ref[...] = exp_out / denom


def _router(h, w_router, b_router, cfg):
    m, d = h.shape
    e = cfg.n_experts
    tm = min(cfg.route_tm, m)
    w_pad = jnp.zeros((d, LANES), F32).at[:, :e].set(w_router)
    b_pad = jnp.full((1, LANES), MASK_VALUE, F32).at[0, :e].set(b_router)
    return pl.pallas_call(
        _router_kernel,
        out_shape=(jax.ShapeDtypeStruct((m, LANES), jnp.int32),
                   jax.ShapeDtypeStruct((m, LANES), F32)),
        grid=(m // tm,),
        in_specs=[pl.BlockSpec((tm, d), lambda i: (i, 0)),
                  pl.BlockSpec((d, LANES), lambda i: (0, 0)),
                  pl.BlockSpec((1, LANES), lambda i: (0, 0))],
        out_specs=(pl.BlockSpec((tm, LANES), lambda i: (i, 0)),
                   pl.BlockSpec((tm, LANES), lambda i: (i, 0))),
        compiler_params=_params(("parallel",)),
        name="router",
    )(h, w_pad, b_pad)


def _row_copy(src_hbm, row, dst_vmem, slot, sem):
    return pltpu.make_async_copy(src_hbm.at[pl.ds(row, 1), :], dst_vmem.at[pl.ds(slot, 1), :], sem)


def _slab_pitch(n_slab):
    return n_slab + 4 if n_slab % 8 == 0 else n_slab


def _slab_copy(src_hbm, row, dst_vmem, slot, sem):
    n_slab = src_hbm.shape[1]
    dst = dst_vmem.at[pl.ds(slot * _slab_pitch(n_slab), n_slab), :]
    return pltpu.make_async_copy(src_hbm.at[row], dst, sem)


def _for_each_row(n, fn):
    groups = n // ROW_UNROLL

    def group(i, c):
        for u in range(ROW_UNROLL):
            fn(i * ROW_UNROLL + u)
        return c

    def single(r, c):
        fn(r)
        return c

    lax.fori_loop(0, groups, group, 0)
    lax.fori_loop(groups * ROW_UNROLL, n, single, 0)


def _row_slabs(x):
    return x.reshape(x.shape[0], x.shape[1] // LANES, LANES)


def _gather_rows_kernel(tok_ref, src_ref, nvalid_ref, h_hbm, o_ref, buf_ref, sem, *, tm):
    blk = pl.program_id(0)

    def copy(b, r):
        slot = b % 2
        return _slab_copy(h_hbm, tok_ref[src_ref[b] + r], buf_ref.at[slot], r, sem.at[slot])

    def issue(b):
        _for_each_row(nvalid_ref[b], lambda r: copy(b, r).start())

    @pl.when(blk == 0)
    def _():
        buf_ref[...] = jnp.zeros_like(buf_ref)
        issue(0)

    @pl.when(blk + 1 < pl.num_programs(0))
    def _():
        issue(blk + 1)

    _for_each_row(nvalid_ref[blk], lambda r: copy(blk, r).wait())
    slot = blk % 2
    valid = lax.broadcasted_iota(jnp.int32, (tm, 1), 0) < nvalid_ref[blk]
    n_slab = h_hbm.shape[1]
    for s in range(n_slab):
        chunk = jnp.where(valid, buf_ref[slot, pl.ds(s, tm, stride=_slab_pitch(n_slab)), :], 0.0)
        o_ref[:, s * LANES:(s + 1) * LANES] = chunk.astype(o_ref.dtype)


def _gather_rows(h_slabs, sorted_tok, blk_src, blk_nvalid, cfg):
    _, n_slab, _ = h_slabs.shape
    tm = cfg.moe_tm
    n_blk = blk_src.shape[0]
    return pl.pallas_call(
        functools.partial(_gather_rows_kernel, tm=tm),
        out_shape=jax.ShapeDtypeStruct((n_blk * tm, n_slab * LANES), BF16),
        grid_spec=pltpu.PrefetchScalarGridSpec(
            num_scalar_prefetch=3, grid=(n_blk,),
            in_specs=[pl.BlockSpec(memory_space=pl.ANY)],
            out_specs=pl.BlockSpec((tm, n_slab * LANES), lambda i, tok, src, nv: (i, 0)),
            scratch_shapes=[pltpu.VMEM((2, tm * _slab_pitch(n_slab), LANES), F32),
                            pltpu.SemaphoreType.DMA((2,))]),
        compiler_params=_params(("arbitrary",)),
        name="moe_gather",
    )(sorted_tok, blk_src, blk_nvalid, h_slabs)


def _expert_up_kernel(blk_e_ref, nused_ref, x_ref, wg_ref, wu_ref, bg_ref, bu_ref, act_ref,
                      wg_bf, wu_bf):
    blk = pl.program_id(1)
    e = blk_e_ref[blk]
    e_prev = blk_e_ref[jnp.maximum(blk - 1, 0)]

    @pl.when((blk == 0) | (e != e_prev))
    def _():
        wg_bf[...] = wg_ref[0].astype(BF16)
        wu_bf[...] = wu_ref[0].astype(BF16)

    @pl.when(blk < nused_ref[0])
    def _():
        x = x_ref[...]
        g = jnp.dot(x, wg_bf[...], preferred_element_type=F32) + bg_ref[0]
        u = jnp.dot(x, wu_bf[...], preferred_element_type=F32) + bu_ref[0]
        g = jnp.minimum(g, SWIGLU_LIMIT)
        u = jnp.clip(u, -SWIGLU_LIMIT, SWIGLU_LIMIT)
        act_ref[...] = (g * jax.nn.sigmoid(SWIGLU_ALPHA * g) * (u + 1.0)).astype(act_ref.dtype)

    @pl.when(blk >= nused_ref[0])
    def _():
        act_ref[...] = jnp.zeros_like(act_ref)


def _expert_up(rows, w_up, b_up, blk_e, n_used, cfg):
    n_rows, d = rows.shape
    d_ff = w_up.shape[2] // 2
    tm, tf = cfg.moe_tm, min(cfg.moe_tf, d_ff)
    nf = d_ff // tf
    n_blk = n_rows // tm
    b_up3 = b_up.reshape(b_up.shape[0], 1, 2 * d_ff)
    return pl.pallas_call(
        _expert_up_kernel,
        out_shape=jax.ShapeDtypeStruct((n_rows, d_ff), BF16),
        grid_spec=pltpu.PrefetchScalarGridSpec(
            num_scalar_prefetch=2, grid=(nf, n_blk),
            in_specs=[pl.BlockSpec((tm, d), lambda f, i, be, nu: (i, 0)),
                      pl.BlockSpec((1, d, tf), lambda f, i, be, nu: (be[i], 0, f)),
                      pl.BlockSpec((1, d, tf), lambda f, i, be, nu: (be[i], 0, nf + f)),
                      pl.BlockSpec((1, 1, tf), lambda f, i, be, nu: (be[i], 0, f)),
                      pl.BlockSpec((1, 1, tf), lambda f, i, be, nu: (be[i], 0, nf + f))],
            out_specs=pl.BlockSpec((tm, tf), lambda f, i, be, nu: (i, f)),
            scratch_shapes=[pltpu.VMEM((d, tf), BF16), pltpu.VMEM((d, tf), BF16)]),
        compiler_params=_params(("arbitrary", "arbitrary")),
        name="moe_up",
    )(blk_e, n_used, rows, w_up, w_up, b_up3, b_up3)


def _expert_down_kernel(blk_e_ref, nused_ref, a_ref, w_ref, b_ref, y_ref, w_bf):
    blk = pl.program_id(1)
    e = blk_e_ref[blk]
    e_prev = blk_e_ref[jnp.maximum(blk - 1, 0)]

    @pl.when((blk == 0) | (e != e_prev))
    def _():
        w_bf[...] = w_ref[0].astype(BF16)

    @pl.when(blk < nused_ref[0])
    def _():
        y_ref[...] = jnp.dot(a_ref[...], w_bf[...], preferred_element_type=F32) + b_ref[0]

    @pl.when(blk >= nused_ref[0])
    def _():
        y_ref[...] = jnp.zeros_like(y_ref)


def _expert_down(act, w_down, b_down, blk_e, n_used, cfg):
    n_rows, d_ff = act.shape
    d = w_down.shape[2]
    tm, tn = cfg.moe_tm, min(cfg.moe_tn, d)
    n_blk = n_rows // tm
    b_down3 = b_down.reshape(b_down.shape[0], 1, d)
    return pl.pallas_call(
        _expert_down_kernel,
        out_shape=jax.ShapeDtypeStruct((n_rows, d), F32),
        grid_spec=pltpu.PrefetchScalarGridSpec(
            num_scalar_prefetch=2, grid=(d // tn, n_blk),
            in_specs=[pl.BlockSpec((tm, d_ff), lambda n, i, be, nu: (i, 0)),
                      pl.BlockSpec((1, d_ff, tn), lambda n, i, be, nu: (be[i], 0, n)),
                      pl.BlockSpec((1, 1, tn), lambda n, i, be, nu: (be[i], 0, n))],
            out_specs=pl.BlockSpec((tm, tn), lambda n, i, be, nu: (i, n)),
            scratch_shapes=[pltpu.VMEM((d_ff, tn), BF16)]),
        compiler_params=_params(("arbitrary", "arbitrary")),
        name="moe_down",
    )(blk_e, n_used, act, w_down, b_down3)


def _combine_kernel(dest_ref, y_hbm, gate_ref, h_ref, g_ref, b_ref, o_ref, buf_ref, sem,
                    *, tt, alpha):
    i = pl.program_id(0)
    n_copy = tt * TOP_K

    def copy(tile, a):
        slot = tile % 2
        return _row_copy(y_hbm, dest_ref[tile * n_copy + a], buf_ref.at[slot], a, sem.at[slot])

    def issue(tile):
        _for_each_row(n_copy, lambda a: copy(tile, a).start())

    @pl.when(i == 0)
    def _():
        issue(0)

    @pl.when(i + 1 < pl.num_programs(0))
    def _():
        issue(i + 1)

    _for_each_row(n_copy, lambda a: copy(i, a).wait())
    slot = i % 2
    gate = gate_ref[...]
    ffn = gate[:, 0:1] * buf_ref[slot, pl.ds(0, tt), :]
    for k in range(1, TOP_K):
        ffn = ffn + gate[:, k:k + 1] * buf_ref[slot, pl.ds(k * tt, tt), :]
    o_ref[...] = _layer_norm(alpha * h_ref[...] + ffn, g_ref[...], b_ref[...])


def _combine(y_slabs, dest_kmajor, gate, h, ln_g, ln_b, cfg, alpha):
    m, d = h.shape
    tt = min(cfg.comb_tt, m)
    return pl.pallas_call(
        functools.partial(_combine_kernel, tt=tt, alpha=alpha),
        out_shape=jax.ShapeDtypeStruct((m, d), F32),
        grid_spec=pltpu.PrefetchScalarGridSpec(
            num_scalar_prefetch=1, grid=(m // tt,),
            in_specs=[pl.BlockSpec(memory_space=pl.ANY),
                      pl.BlockSpec((tt, LANES), lambda i, dst: (i, 0)),
                      pl.BlockSpec((tt, d), lambda i, dst: (i, 0)),
                      pl.BlockSpec((1, d), lambda i, dst: (0, 0)),
                      pl.BlockSpec((1, d), lambda i, dst: (0, 0))],
            out_specs=pl.BlockSpec((tt, d), lambda i, dst: (i, 0)),
            scratch_shapes=[pltpu.VMEM((2, tt * TOP_K, d), F32), pltpu.SemaphoreType.DMA((2,))]),
        compiler_params=_params(("arbitrary",)),
        name="moe_combine",
    )(dest_kmajor, y_slabs, gate, h, ln_g, ln_b)


def _routing_tables(top_idx, cfg):
    n_tok = top_idx.shape[0]
    e, tm = cfg.n_experts, cfg.moe_tm
    n_assign = n_tok * TOP_K
    n_blk = -(-n_assign // tm) + e
    flat_e = top_idx.reshape(-1)
    assign = jnp.arange(n_assign, dtype=jnp.int32)
    sorted_e, sorted_a = lax.sort((flat_e, assign), num_keys=2)
    experts = jnp.arange(e, dtype=jnp.int32)
    onehot = sorted_e[:, None] == experts[None, :]
    counts = jnp.sum(onehot, axis=0, dtype=jnp.int32)
    start = jnp.cumsum(counts) - counts
    blocks_e = (counts + tm - 1) // tm
    blk_end = jnp.cumsum(blocks_e)
    blk_start = blk_end - blocks_e
    shift = blk_start * tm - start
    dest_sorted = assign + jnp.sum(jnp.where(onehot, shift[None, :], 0), axis=1, dtype=jnp.int32)
    _, dest = lax.sort((sorted_a, dest_sorted), num_keys=1)
    blk = jnp.arange(n_blk, dtype=jnp.int32)
    blk_onehot = (blk[:, None] >= blk_start[None, :]) & (blk[:, None] < blk_end[None, :])
    blk_e = jnp.sum(jnp.where(blk_onehot, experts[None, :], 0), axis=1, dtype=jnp.int32)
    n_used = blk_end[-1:].astype(jnp.int32)
    blk_e = jnp.where(blk < n_used[0], blk_e, e - 1)
    local = (blk - jnp.sum(jnp.where(blk_onehot, blk_start[None, :], 0), axis=1)) * tm
    blk_count = jnp.sum(jnp.where(blk_onehot, counts[None, :], 0), axis=1)
    blk_src = (jnp.sum(jnp.where(blk_onehot, start[None, :], 0), axis=1) + local).astype(jnp.int32)
    blk_nvalid = jnp.clip(blk_count - local, 0, tm).astype(jnp.int32)
    sorted_tok = sorted_a // TOP_K
    return dest.astype(jnp.int32), blk_e, n_used, sorted_tok, blk_src, blk_nvalid


def _moe(h, w_router, b_router, w_up, b_up, w_down, b_down, ln_g, ln_b, cfg, alpha):
    n_tok = h.shape[0]
    idx_pad, gate_pad = _router(h, w_router, b_router, cfg)
    dest, blk_e, n_used, sorted_tok, blk_src, blk_nvalid = _routing_tables(idx_pad[:, :TOP_K], cfg)
    rows = _gather_rows(_row_slabs(h), sorted_tok, blk_src, blk_nvalid, cfg)
    act = _expert_up(rows, w_up, b_up, blk_e, n_used, cfg)
    y_slabs = _expert_down(act, w_down, b_down, blk_e, n_used, cfg)
    tt = min(cfg.comb_tt, n_tok)
    dest_kmajor = dest.reshape(n_tok // tt, tt, TOP_K).transpose(0, 2, 1).reshape(-1)
    return _combine(y_slabs, dest_kmajor, gate_pad, h, ln_g, ln_b, cfg, alpha)


def _layer(h, w_in, b_in, w_sb, w_fx, w_out, ln1_g, ln1_b, w_router, b_router,
           w_up, b_up, w_down, b_down, ln2_g, ln2_b, cfg):
    b, s, d = h.shape
    heads = cfg.heads
    width = heads * HEAD_DIM
    alpha = (2.0 * cfg.depth) ** 0.25
    n_tok = b * s
    x = h.reshape(n_tok, d)
    x_bf = x.astype(BF16)
    o_f = 6 * width
    o_g = o_f + heads
    qkv = _matmul(x_bf, w_in[:, :o_f].astype(BF16), b_in[None, :o_f], out_dtype=BF16, act=None,
                  tm=cfg.mm_tm, tn=cfg.mm_tn, name="in_proj_qkv")
    gates = _matmul(x_bf, w_in[:, o_g:].astype(BF16), b_in[None, o_g:], out_dtype=BF16,
                    act="sigmoid", tm=cfg.mm_tm, tn=cfg.mm_tn, name="in_proj_gates")
    w_f = jnp.zeros((d, LANES), BF16).at[:, :heads].set(w_in[:, o_f:o_g].astype(BF16))
    b_f = jnp.zeros((1, LANES), F32).at[0, :heads].set(b_in[o_f:o_g])
    f_logit = _matmul(x_bf, w_f, b_f, out_dtype=F32, act=None,
                      tm=cfg.mm_tm, tn=LANES, name="in_proj_forget")
    f_rows = f_logit[:, :heads].reshape(b, s, heads).transpose(0, 2, 1).reshape(b * heads, s)
    c_rows = _forget_cumsum(f_rows).reshape(b, heads, 1, s)

    qkv3 = qkv.reshape(b, s, o_f)
    o_sb = _attention(qkv3, None, cfg, "sb").reshape(n_tok, width)
    o_fx = _attention(qkv3, c_rows, cfg, "fx").reshape(n_tok, width)

    h1 = _post_attention(o_sb, o_fx, gates, w_sb.astype(BF16), w_fx.astype(BF16),
                         w_out.astype(BF16), x, ln1_g[None, :], ln1_b[None, :], cfg, alpha)
    out = _moe(h1, w_router, b_router, w_up, b_up, w_down, b_down,
               ln2_g[None, :], ln2_b[None, :], cfg, alpha)
    return out.reshape(b, s, d)


def kernel(x, w_in, b_in, w_branch_sb, w_branch_fx, w_out, ln1_g, ln1_b, w_router, b_router,
           w_up, b_up, w_down, b_down, ln2_g, ln2_b):
    depth = w_in.shape[0]
    d = x.shape[-1]
    n_experts = w_router.shape[-1]
    heads = (w_in.shape[-1] - 2 * d) // (6 * HEAD_DIM + 1)
    cfg = Config(heads=heads, n_experts=n_experts, depth=depth,
                 mm_tm=1024, mm_tn=1024, attn_tile=256, attn_group=4, post_tm=512, post_tc=512,
                 route_tm=512, moe_tm=512, moe_tf=512, moe_tn=2048, comb_tt=128)
    h = x
    for l in range(depth):
        h = _layer(h, w_in[l], b_in[l], w_branch_sb[l], w_branch_fx[l], w_out[l],
                   ln1_g[l], ln1_b[l], w_router[l], b_router[l], w_up[l], b_up[l],
                   w_down[l], b_down[l], ln2_g[l], ln2_b[l], cfg)
    return h
```

```python
import functools
from typing import NamedTuple

import jax
import jax.numpy as jnp
from jax import lax
from jax.experimental import pallas as pl
from jax.experimental.pallas import tpu as pltpu

F32 = jnp.float32
BF16 = jnp.bfloat16

LANES = 128
HEAD_DIM = 128
TOP_K = 4
SWIGLU_ALPHA = 1.702
SWIGLU_LIMIT = 7.0
LN_EPS = 1e-5
LOG2_E = 1.4426950408889634
MASK_VALUE = -1e30
SB_EXIT_LOG2 = -160.0
ROW_UNROLL = 8
VMEM_LIMIT_BYTES = 60 * 1024 * 1024


class Config(NamedTuple):
    heads: int
    n_experts: int
    depth: int
    mm_tm: int
    mm_tn: int
    attn_tile: int
    attn_group: int
    post_tm: int
    post_tc: int
    route_tm: int
    moe_tm: int
    moe_tf: int
    moe_tn: int
    comb_tt: int


def _params(sem):
    return pltpu.CompilerParams(dimension_semantics=sem, vmem_limit_bytes=VMEM_LIMIT_BYTES)


def _matmul_kernel(x_ref, w_ref, b_ref, o_ref, *, act):
    acc = jnp.dot(x_ref[...], w_ref[...], preferred_element_type=F32) + b_ref[...]
    if act == "sigmoid":
        acc = jax.nn.sigmoid(acc)
    o_ref[...] = acc.astype(o_ref.dtype)


def _matmul(x, w, b, *, out_dtype, act, tm, tn, name):
    m, k = x.shape
    n = w.shape[1]
    tm, tn = min(tm, m), min(tn, n)
    return pl.pallas_call(
        functools.partial(_matmul_kernel, act=act),
        out_shape=jax.ShapeDtypeStruct((m, n), out_dtype),
        grid=(m // tm, n // tn),
        in_specs=[pl.BlockSpec((tm, k), lambda i, j: (i, 0)),
                  pl.BlockSpec((k, tn), lambda i, j: (0, j)),
                  pl.BlockSpec((1, tn), lambda i, j: (0, j))],
        out_specs=pl.BlockSpec((tm, tn), lambda i, j: (i, j)),
        compiler_params=_params(("parallel", "arbitrary")),
        name=name,
    )(x, w, b)


def _forget_cumsum_kernel(f_ref, c_ref):
    f = f_ref[...]
    x = jnp.minimum(f, 0.0) - jnp.log(1.0 + jnp.exp(-jnp.abs(f)))
    n = x.shape[1]
    lane = lax.broadcasted_iota(jnp.int32, x.shape, 1)
    d = 1
    while d < n:
        x = x + jnp.where(lane >= d, pltpu.roll(x, d, axis=1), 0.0)
        d *= 2
    c_ref[...] = x


def _forget_cumsum(f_logit_rows):
    return pl.pallas_call(
        _forget_cumsum_kernel,
        out_shape=jax.ShapeDtypeStruct(f_logit_rows.shape, F32),
        name="forget_cumsum",
    )(f_logit_rows)


def _qk(q, k):
    return lax.dot_general(q, k, (((1,), (1,)), ((), ())), preferred_element_type=F32)


def _lane_repeat(x, width):
    return jnp.concatenate([x] * (width // LANES), axis=1)


def _neg_abs(x):
    bits = pltpu.bitcast(x, jnp.uint32) | jnp.uint32(0x80000000)
    return pltpu.bitcast(bits, F32)


def _sb_attn_kernel(q_ref, k_ref, v_ref, o_ref, u_ref, acc_ref, carry_ref, z_ref, *, tile, scale,
                    group):
    i = pl.program_id(2)
    t = tile
    q = q_ref[0]
    row = lax.broadcasted_iota(jnp.int32, (t, t), 0)
    col = lax.broadcasted_iota(jnp.int32, (t, t), 1)
    strict = col < row
    u_ref[...] = (row > col).astype(BF16)
    acc_ref[...] = jnp.zeros_like(acc_ref)
    carry_ref[...] = jnp.zeros_like(carry_ref)

    heads = [slice(g * HEAD_DIM, (g + 1) * HEAD_DIM) for g in range(group)]

    def scores(j):
        kj = k_ref[0, pl.ds(pl.multiple_of(j * t, t), t), :]
        return [_qk(q[:, hd], kj[:, hd]) for hd in heads]

    def step(j, masked):
        z_next = scores(jnp.maximum(j - 1, 0))
        if masked:
            z_raw = scores(j)
        else:
            z_raw = [z_ref[:, g * t:(g + 1) * t] for g in range(group)]
        vj = v_ref[0, pl.ds(pl.multiple_of(j * t, t), t), :]
        u = u_ref[...]
        carry = carry_ref[...]
        z = [zg * (scale * LOG2_E) for zg in z_raw]
        log_beta, log_keep, row_sum = [], [], []
        for zg in z:
            lb = jnp.minimum(zg, 0.0) - jnp.log2(1.0 + jnp.exp2(_neg_abs(zg)))
            lk = lb - zg
            if masked:
                lk = jnp.where(strict, lk, 0.0)
            log_beta.append(lb)
            log_keep.append(lk.astype(BF16))
            row_sum.append(jnp.broadcast_to(jnp.sum(lk, axis=1, keepdims=True), (t, LANES)))
        after = [jnp.dot(lk, u, preferred_element_type=F32) for lk in log_keep]
        w = []
        for hd, lb, af in zip(heads, log_beta, after):
            wg = jnp.exp2(lb + af + _lane_repeat(carry[:, hd], t))
            if masked:
                wg = jnp.where(strict, wg, 0.0)
            w.append(wg.astype(BF16))
        pv = [jnp.dot(wg, vj[:, hd], preferred_element_type=F32) for hd, wg in zip(heads, w)]
        acc_ref[...] += jnp.concatenate(pv, axis=1)
        carry_ref[...] = carry + jnp.concatenate(row_sum, axis=1)
        z_ref[...] = jnp.concatenate(z_next, axis=1)

    step(i, True)

    def alive():
        return jnp.max(carry_ref[...]) > SB_EXIT_LOG2

    def cond(state):
        it, go = state
        return (it < i) & go

    def body(state):
        it, _ = state
        step(i - 1 - it, False)
        return it + 1, alive()

    lax.while_loop(cond, body, (jnp.int32(0), alive()))
    o_ref[0] = acc_ref[...].astype(o_ref.dtype)


def _fx_attn_kernel(q_ref, k_ref, v_ref, c_ref, o_ref, m_ref, acc_ref, z_ref, *, tile, scale,
                    group):
    i = pl.program_id(2)
    t = tile
    q = q_ref[0]
    row = lax.broadcasted_iota(jnp.int32, (t, t), 0)
    col = lax.broadcasted_iota(jnp.int32, (t, t), 1)
    causal = col <= row
    m_ref[...] = jnp.full_like(m_ref, MASK_VALUE)
    acc_ref[...] = jnp.zeros_like(acc_ref)
    ones = jnp.ones((t, LANES), BF16)
    heads = [slice(g * HEAD_DIM, (g + 1) * HEAD_DIM) for g in range(group)]

    def scores(j):
        kj = k_ref[0, pl.ds(pl.multiple_of(j * t, t), t), :]
        return jnp.concatenate([_qk(q[:, hd], kj[:, hd]) for hd in heads], axis=1)

    def step(j, masked):
        if not masked:
            z_next = scores(j + 1)
        start = pl.multiple_of(j * t, t)
        vj = v_ref[0, pl.ds(start, t), :]
        m_all = m_ref[...]
        p, rescale, m_new = [], [], []
        for g, hd in enumerate(heads):
            s = (z_ref[:, g * t:(g + 1) * t] * (scale * LOG2_E)
                 - c_ref[0, g, :, pl.ds(start, t)] * LOG2_E)
            if masked:
                s = jnp.where(causal, s, MASK_VALUE)
            m_prev = m_all[:, hd]
            m_next = jnp.maximum(m_prev, jnp.max(s, axis=1, keepdims=True))
            p.append(jnp.exp2(s - _lane_repeat(m_next, t)).astype(BF16))
            rescale.append(_lane_repeat(jnp.exp2(m_prev - m_next), 2 * LANES))
            m_new.append(m_next)
        pv = [jnp.dot(pg, jnp.concatenate([vj[:, hd], ones], axis=1), preferred_element_type=F32)
              for hd, pg in zip(heads, p)]
        acc_ref[...] = jnp.concatenate(rescale, axis=1) * acc_ref[...] + jnp.concatenate(pv, axis=1)
        m_ref[...] = jnp.concatenate(m_new, axis=1)
        if not masked:
            z_ref[...] = z_next

    def body(j, c):
        step(j, False)
        return c

    z_ref[...] = scores(0)
    lax.fori_loop(0, i, body, 0)
    step(i, True)
    for g in range(group):
        acc = acc_ref[:, 2 * g * LANES:2 * (g + 1) * LANES]
        o_ref[0, :, g * HEAD_DIM:(g + 1) * HEAD_DIM] = (
            acc[:, :LANES] / acc[:, LANES:]).astype(o_ref.dtype)


def _attention(qkv, c_rows, cfg, which):
    b, s, _ = qkv.shape
    h = cfg.heads
    grp = cfg.attn_group
    ng = h // grp
    t = min(cfg.attn_tile, s)
    base = 0 if which == "sb" else 3 * ng
    scale = HEAD_DIM ** -0.5
    gw = grp * HEAD_DIM
    q_spec = pl.BlockSpec((1, t, gw), lambda bi, hi, qi: (bi, qi, base + hi))
    k_spec = pl.BlockSpec((1, s, gw), lambda bi, hi, qi: (bi, 0, base + ng + hi))
    v_spec = pl.BlockSpec((1, s, gw), lambda bi, hi, qi: (bi, 0, base + 2 * ng + hi))
    o_spec = pl.BlockSpec((1, t, gw), lambda bi, hi, qi: (bi, qi, hi))
    out_shape = jax.ShapeDtypeStruct((b, s, h * HEAD_DIM), BF16)
    grid = (b, ng, s // t)
    sem = ("parallel", "parallel", "arbitrary")
    if which == "sb":
        return pl.pallas_call(
            functools.partial(_sb_attn_kernel, tile=t, scale=scale, group=grp),
            out_shape=out_shape, grid=grid,
            in_specs=[q_spec, k_spec, v_spec], out_specs=o_spec,
            scratch_shapes=[pltpu.VMEM((t, t), BF16),
                            pltpu.VMEM((t, gw), F32),
                            pltpu.VMEM((t, grp * LANES), F32),
                            pltpu.VMEM((t, grp * t), F32)],
            compiler_params=_params(sem), name="sb_attention",
        )(qkv, qkv, qkv)
    c_spec = pl.BlockSpec((1, grp, 1, s), lambda bi, hi, qi: (bi, hi, 0, 0))
    return pl.pallas_call(
        functools.partial(_fx_attn_kernel, tile=t, scale=scale, group=grp),
        out_shape=out_shape, grid=grid,
        in_specs=[q_spec, k_spec, v_spec, c_spec], out_specs=o_spec,
        scratch_shapes=[pltpu.VMEM((t, grp * LANES), F32),
                        pltpu.VMEM((t, grp * 2 * LANES), F32),
                        pltpu.VMEM((t, grp * t), F32)],
        compiler_params=_params(sem), name="fx_attention",
    )(qkv, qkv, qkv, c_rows)


def _layer_norm(y, g, b):
    mu = jnp.mean(y, axis=-1, keepdims=True)
    d = y - mu
    var = jnp.mean(d * d, axis=-1, keepdims=True)
    return d * lax.rsqrt(var + LN_EPS) * g + b


def _post_attn_kernel(osb_ref, ofx_ref, gsb_ref, gfx_ref, wsb_ref, wfx_ref, wout_ref,
                      x_hbm, g_ref, b_ref, h_ref, x_buf, sem, *, alpha, tm):
    i = pl.program_id(0)
    c = pl.program_id(1)
    x_copy = pltpu.make_async_copy(x_hbm.at[pl.ds(i * tm, tm), :], x_buf, sem)

    @pl.when(c == 0)
    def _():
        x_copy.start()
        h_ref[...] = jnp.zeros_like(h_ref)

    tsb = jnp.dot(osb_ref[...], wsb_ref[...], preferred_element_type=F32)
    tfx = jnp.dot(ofx_ref[...], wfx_ref[...], preferred_element_type=F32)
    merged = gsb_ref[...].astype(F32) * tsb + gfx_ref[...].astype(F32) * tfx
    h_ref[...] += jnp.dot(merged.astype(BF16), wout_ref[...], preferred_element_type=F32)

    @pl.when(c == pl.num_programs(1) - 1)
    def _():
        x_copy.wait()
        h_ref[...] = _layer_norm(alpha * x_buf[...] + h_ref[...], g_ref[...], b_ref[...])


def _post_attention(o_sb, o_fx, gates, w_sb, w_fx, w_out, x, ln_g, ln_b, cfg, alpha):
    m, d = x.shape
    width = o_sb.shape[1]
    tm, tc = min(cfg.post_tm, m), min(cfg.post_tc, d)
    nc = d // tc
    return pl.pallas_call(
        functools.partial(_post_attn_kernel, alpha=alpha, tm=tm),
        out_shape=jax.ShapeDtypeStruct((m, d), F32),
        grid=(m // tm, nc),
        in_specs=[pl.BlockSpec((tm, width), lambda i, c: (i, 0)),
                  pl.BlockSpec((tm, width), lambda i, c: (i, 0)),
                  pl.BlockSpec((tm, tc), lambda i, c: (i, c)),
                  pl.BlockSpec((tm, tc), lambda i, c: (i, nc + c)),
                  pl.BlockSpec((width, tc), lambda i, c: (0, c)),
                  pl.BlockSpec((width, tc), lambda i, c: (0, c)),
                  pl.BlockSpec((tc, d), lambda i, c: (c, 0)),
                  pl.BlockSpec(memory_space=pl.ANY),
                  pl.BlockSpec((1, d), lambda i, c: (0, 0)),
                  pl.BlockSpec((1, d), lambda i, c: (0, 0))],
        out_specs=pl.BlockSpec((tm, d), lambda i, c: (i, 0)),
        scratch_shapes=[pltpu.VMEM((tm, d), F32), pltpu.SemaphoreType.DMA],
        compiler_params=_params(("arbitrary", "arbitrary")),
        name="post_attention",
    )(o_sb, o_fx, gates, gates, w_sb, w_fx, w_out, x, ln_g, ln_b)


def _router_kernel(h_ref, w_ref, b_ref, idx_ref, gate_ref):
    logits = jnp.dot(h_ref[...], w_ref[...], preferred_element_type=F32,
                     precision=lax.Precision.HIGHEST) + b_ref[...]
    lane = lax.broadcasted_iota(jnp.int32, logits.shape, 1)
    lane_f = lane.astype(F32)
    vals = logits
    idx_out = jnp.zeros(logits.shape, F32)
    exp_out = jnp.zeros(logits.shape, F32)
    denom = jnp.zeros((logits.shape[0], 1), F32)
    top = None
    for k in range(TOP_K):
        m = jnp.max(vals, axis=1, keepdims=True)
        sel = jnp.min(jnp.where(vals == m, lane_f, float(LANES)), axis=1, keepdims=True)
        if k == 0:
            top = m
        e = jnp.exp(m - top)
        denom = denom + e
        idx_out = jnp.where(lane == k, sel, idx_out)
        exp_out = jnp.where(lane == k, e, exp_out)
        vals = jnp.where(lane_f == sel, MASK_VALUE * 2.0, vals)
    idx_ref[...] = idx_out.astype(jnp.int32)
    gate_ref[...] = exp_out / denom


def _router(h, w_router, b_router, cfg):
    m, d = h.shape
    e = cfg.n_experts
    tm = min(cfg.route_tm, m)
    w_pad = jnp.zeros((d, LANES), F32).at[:, :e].set(w_router)
    b_pad = jnp.full((1, LANES), MASK_VALUE, F32).at[0, :e].set(b_router)
    return pl.pallas_call(
        _router_kernel,
        out_shape=(jax.ShapeDtypeStruct((m, LANES), jnp.int32),
                   jax.ShapeDtypeStruct((m, LANES), F32)),
        grid=(m // tm,),
        in_specs=[pl.BlockSpec((tm, d), lambda i: (i, 0)),
                  pl.BlockSpec((d, LANES), lambda i: (0, 0)),
                  pl.BlockSpec((1, LANES), lambda i: (0, 0))],
        out_specs=(pl.BlockSpec((tm, LANES), lambda i: (i, 0)),
                   pl.BlockSpec((tm, LANES), lambda i: (i, 0))),
        compiler_params=_params(("parallel",)),
        name="router",
    )(h, w_pad, b_pad)


def _row_copy(src_hbm, row, dst_vmem, slot, sem):
    return pltpu.make_async_copy(src_hbm.at[pl.ds(row, 1), :], dst_vmem.at[pl.ds(slot, 1), :], sem)


def _slab_pitch(n_slab):
    return n_slab + 4 if n_slab % 8 == 0 else n_slab


def _slab_copy(src_hbm, row, dst_vmem, slot, sem):
    n_slab = src_hbm.shape[1]
    dst = dst_vmem.at[pl.ds(slot * _slab_pitch(n_slab), n_slab), :]
    return pltpu.make_async_copy(src_hbm.at[row], dst, sem)


def _for_each_row(n, fn):
    groups = n // ROW_UNROLL

    def group(i, c):
        for u in range(ROW_UNROLL):
            fn(i * ROW_UNROLL + u, u % 2)
        return c

    def single(r, c):
        fn(r, 0)
        return c

    lax.fori_loop(0, groups, group, 0)
    lax.fori_loop(groups * ROW_UNROLL, n, single, 0)


def _row_slabs(x):
    return x.reshape(x.shape[0], x.shape[1] // LANES, LANES)


def _gather_rows_kernel(tok_ref, src_ref, nvalid_ref, h_hbm, o_ref, buf_ref, sem, *, tm):
    blk = pl.program_id(0)

    def copy(b, r):
        slot = b % 2
        return _slab_copy(h_hbm, tok_ref[src_ref[b] + r], buf_ref.at[slot], r, sem.at[slot])

    def issue(b):
        _for_each_row(nvalid_ref[b], lambda r, lane: copy(b, r).start(priority=lane))

    @pl.when(blk == 0)
    def _():
        buf_ref[...] = jnp.zeros_like(buf_ref)
        issue(0)

    @pl.when(blk + 1 < pl.num_programs(0))
    def _():
        issue(blk + 1)

    _for_each_row(nvalid_ref[blk], lambda r, lane: copy(blk, r).wait())
    slot = blk % 2
    valid = lax.broadcasted_iota(jnp.int32, (tm, 1), 0) < nvalid_ref[blk]
    n_slab = h_hbm.shape[1]
    for s in range(n_slab):
        chunk = jnp.where(valid, buf_ref[slot, pl.ds(s, tm, stride=_slab_pitch(n_slab)), :], 0.0)
        o_ref[:, s * LANES:(s + 1) * LANES] = chunk.astype(o_ref.dtype)


def _gather_rows(h_slabs, sorted_tok, blk_src, blk_nvalid, cfg):
    _, n_slab, _ = h_slabs.shape
    tm = cfg.moe_tm
    n_blk = blk_src.shape[0]
    return pl.pallas_call(
        functools.partial(_gather_rows_kernel, tm=tm),
        out_shape=jax.ShapeDtypeStruct((n_blk * tm, n_slab * LANES), BF16),
        grid_spec=pltpu.PrefetchScalarGridSpec(
            num_scalar_prefetch=3, grid=(n_blk,),
            in_specs=[pl.BlockSpec(memory_space=pl.ANY)],
            out_specs=pl.BlockSpec((tm, n_slab * LANES), lambda i, tok, src, nv: (i, 0)),
            scratch_shapes=[pltpu.VMEM((2, tm * _slab_pitch(n_slab), LANES), F32),
                            pltpu.SemaphoreType.DMA((2,))]),
        compiler_params=_params(("arbitrary",)),
        name="moe_gather",
    )(sorted_tok, blk_src, blk_nvalid, h_slabs)


def _expert_up_kernel(blk_e_ref, nused_ref, x_ref, wg_ref, wu_ref, bg_ref, bu_ref, act_ref,
                      wg_bf, wu_bf):
    blk = pl.program_id(1)
    e = blk_e_ref[blk]
    e_prev = blk_e_ref[jnp.maximum(blk - 1, 0)]

    @pl.when((blk == 0) | (e != e_prev))
    def _():
        wg_bf[...] = wg_ref[0].astype(BF16)
        wu_bf[...] = wu_ref[0].astype(BF16)

    @pl.when(blk < nused_ref[0])
    def _():
        x = x_ref[...]
        g = jnp.dot(x, wg_bf[...], preferred_element_type=F32) + bg_ref[0]
        u = jnp.dot(x, wu_bf[...], preferred_element_type=F32) + bu_ref[0]
        g = jnp.minimum(g, SWIGLU_LIMIT)
        u = jnp.clip(u, -SWIGLU_LIMIT, SWIGLU_LIMIT)
        act_ref[...] = (g * jax.nn.sigmoid(SWIGLU_ALPHA * g) * (u + 1.0)).astype(act_ref.dtype)

    @pl.when(blk >= nused_ref[0])
    def _():
        act_ref[...] = jnp.zeros_like(act_ref)


def _expert_up(rows, w_up, b_up, blk_e, n_used, cfg):
    n_rows, d = rows.shape
    d_ff = w_up.shape[2] // 2
    tm, tf = cfg.moe_tm, min(cfg.moe_tf, d_ff)
    nf = d_ff // tf
    n_blk = n_rows // tm
    b_up3 = b_up.reshape(b_up.shape[0], 1, 2 * d_ff)
    return pl.pallas_call(
        _expert_up_kernel,
        out_shape=jax.ShapeDtypeStruct((n_rows, d_ff), BF16),
        grid_spec=pltpu.PrefetchScalarGridSpec(
            num_scalar_prefetch=2, grid=(nf, n_blk),
            in_specs=[pl.BlockSpec((tm, d), lambda f, i, be, nu: (i, 0)),
                      pl.BlockSpec((1, d, tf), lambda f, i, be, nu: (be[i], 0, f)),
                      pl.BlockSpec((1, d, tf), lambda f, i, be, nu: (be[i], 0, nf + f)),
                      pl.BlockSpec((1, 1, tf), lambda f, i, be, nu: (be[i], 0, f)),
                      pl.BlockSpec((1, 1, tf), lambda f, i, be, nu: (be[i], 0, nf + f))],
            out_specs=pl.BlockSpec((tm, tf), lambda f, i, be, nu: (i, f)),
            scratch_shapes=[pltpu.VMEM((d, tf), BF16), pltpu.VMEM((d, tf), BF16)]),
        compiler_params=_params(("arbitrary", "arbitrary")),
        name="moe_up",
    )(blk_e, n_used, rows, w_up, w_up, b_up3, b_up3)


def _expert_down_kernel(blk_e_ref, nused_ref, a_ref, w_ref, b_ref, y_ref, w_bf):
    blk = pl.program_id(1)
    e = blk_e_ref[blk]
    e_prev = blk_e_ref[jnp.maximum(blk - 1, 0)]

    @pl.when((blk == 0) | (e != e_prev))
    def _():
        w_bf[...] = w_ref[0].astype(BF16)

    @pl.when(blk < nused_ref[0])
    def _():
        y_ref[...] = jnp.dot(a_ref[...], w_bf[...], preferred_element_type=F32) + b_ref[0]

    @pl.when(blk >= nused_ref[0])
    def _():
        y_ref[...] = jnp.zeros_like(y_ref)


def _expert_down(act, w_down, b_down, blk_e, n_used, cfg):
    n_rows, d_ff = act.shape
    d = w_down.shape[2]
    tm, tn = cfg.moe_tm, min(cfg.moe_tn, d)
    n_blk = n_rows // tm
    b_down3 = b_down.reshape(b_down.shape[0], 1, d)
    return pl.pallas_call(
        _expert_down_kernel,
        out_shape=jax.ShapeDtypeStruct((n_rows, d), F32),
        grid_spec=pltpu.PrefetchScalarGridSpec(
            num_scalar_prefetch=2, grid=(d // tn, n_blk),
            in_specs=[pl.BlockSpec((tm, d_ff), lambda n, i, be, nu: (i, 0)),
                      pl.BlockSpec((1, d_ff, tn), lambda n, i, be, nu: (be[i], 0, n)),
                      pl.BlockSpec((1, 1, tn), lambda n, i, be, nu: (be[i], 0, n))],
            out_specs=pl.BlockSpec((tm, tn), lambda n, i, be, nu: (i, n)),
            scratch_shapes=[pltpu.VMEM((d_ff, tn), BF16)]),
        compiler_params=_params(("arbitrary", "arbitrary")),
        name="moe_down",
    )(blk_e, n_used, act, w_down, b_down3)


def _combine_kernel(dest_ref, y_hbm, gate_ref, h_ref, g_ref, b_ref, o_ref, buf_ref, sem,
                    *, tt, alpha):
    i = pl.program_id(0)
    n_copy = tt * TOP_K

    def copy(tile, a):
        slot = tile % 2
        return _row_copy(y_hbm, dest_ref[tile * n_copy + a], buf_ref.at[slot], a, sem.at[slot])

    def issue(tile):
        _for_each_row(n_copy, lambda a, lane: copy(tile, a).start(priority=lane))

    @pl.when(i == 0)
    def _():
        issue(0)

    @pl.when(i + 1 < pl.num_programs(0))
    def _():
        issue(i + 1)

    _for_each_row(n_copy, lambda a, lane: copy(i, a).wait())
    slot = i % 2
    gate = gate_ref[...]
    ffn = gate[:, 0:1] * buf_ref[slot, pl.ds(0, tt), :]
    for k in range(1, TOP_K):
        ffn = ffn + gate[:, k:k + 1] * buf_ref[slot, pl.ds(k * tt, tt), :]
    o_ref[...] = _layer_norm(alpha * h_ref[...] + ffn, g_ref[...], b_ref[...])


def _combine(y_slabs, dest_kmajor, gate, h, ln_g, ln_b, cfg, alpha):
    m, d = h.shape
    tt = min(cfg.comb_tt, m)
    return pl.pallas_call(
        functools.partial(_combine_kernel, tt=tt, alpha=alpha),
        out_shape=jax.ShapeDtypeStruct((m, d), F32),
        grid_spec=pltpu.PrefetchScalarGridSpec(
            num_scalar_prefetch=1, grid=(m // tt,),
            in_specs=[pl.BlockSpec(memory_space=pl.ANY),
                      pl.BlockSpec((tt, LANES), lambda i, dst: (i, 0)),
                      pl.BlockSpec((tt, d), lambda i, dst: (i, 0)),
                      pl.BlockSpec((1, d), lambda i, dst: (0, 0)),
                      pl.BlockSpec((1, d), lambda i, dst: (0, 0))],
            out_specs=pl.BlockSpec((tt, d), lambda i, dst: (i, 0)),
            scratch_shapes=[pltpu.VMEM((2, tt * TOP_K, d), F32), pltpu.SemaphoreType.DMA((2,))]),
        compiler_params=_params(("arbitrary",)),
        name="moe_combine",
    )(dest_kmajor, y_slabs, gate, h, ln_g, ln_b)


def _routing_tables(top_idx, cfg):
    n_tok = top_idx.shape[0]
    e, tm = cfg.n_experts, cfg.moe_tm
    n_assign = n_tok * TOP_K
    n_blk = -(-n_assign // tm) + e
    flat_e = top_idx.reshape(-1)
    assign = jnp.arange(n_assign, dtype=jnp.int32)
    sorted_e, sorted_a = lax.sort((flat_e, assign), num_keys=2)
    experts = jnp.arange(e, dtype=jnp.int32)
    onehot = sorted_e[:, None] == experts[None, :]
    counts = jnp.sum(onehot, axis=0, dtype=jnp.int32)
    start = jnp.cumsum(counts) - counts
    blocks_e = (counts + tm - 1) // tm
    blk_end = jnp.cumsum(blocks_e)
    blk_start = blk_end - blocks_e
    shift = blk_start * tm - start
    dest_sorted = assign + jnp.sum(jnp.where(onehot, shift[None, :], 0), axis=1, dtype=jnp.int32)
    _, dest = lax.sort((sorted_a, dest_sorted), num_keys=1)
    blk = jnp.arange(n_blk, dtype=jnp.int32)
    blk_onehot = (blk[:, None] >= blk_start[None, :]) & (blk[:, None] < blk_end[None, :])
    blk_e = jnp.sum(jnp.where(blk_onehot, experts[None, :], 0), axis=1, dtype=jnp.int32)
    n_used = blk_end[-1:].astype(jnp.int32)
    blk_e = jnp.where(blk < n_used[0], blk_e, e - 1)
    local = (blk - jnp.sum(jnp.where(blk_onehot, blk_start[None, :], 0), axis=1)) * tm
    blk_count = jnp.sum(jnp.where(blk_onehot, counts[None, :], 0), axis=1)
    blk_src = (jnp.sum(jnp.where(blk_onehot, start[None, :], 0), axis=1) + local).astype(jnp.int32)
    blk_nvalid = jnp.clip(blk_count - local, 0, tm).astype(jnp.int32)
    sorted_tok = sorted_a // TOP_K
    return dest.astype(jnp.int32), blk_e, n_used, sorted_tok, blk_src, blk_nvalid


def _moe(h, w_router, b_router, w_up, b_up, w_down, b_down, ln_g, ln_b, cfg, alpha):
    n_tok = h.shape[0]
    idx_pad, gate_pad = _router(h, w_router, b_router, cfg)
    dest, blk_e, n_used, sorted_tok, blk_src, blk_nvalid = _routing_tables(idx_pad[:, :TOP_K], cfg)
    rows = _gather_rows(_row_slabs(h), sorted_tok, blk_src, blk_nvalid, cfg)
    act = _expert_up(rows, w_up, b_up, blk_e, n_used, cfg)
    y_slabs = _expert_down(act, w_down, b_down, blk_e, n_used, cfg)
    tt = min(cfg.comb_tt, n_tok)
    dest_kmajor = dest.reshape(n_tok // tt, tt, TOP_K).transpose(0, 2, 1).reshape(-1)
    return _combine(y_slabs, dest_kmajor, gate_pad, h, ln_g, ln_b, cfg, alpha)


def _layer(h, w_in, b_in, w_sb, w_fx, w_out, ln1_g, ln1_b, w_router, b_router,
           w_up, b_up, w_down, b_down, ln2_g, ln2_b, cfg):
    b, s, d = h.shape
    heads = cfg.heads
    width = heads * HEAD_DIM
    alpha = (2.0 * cfg.depth) ** 0.25
    n_tok = b * s
    x = h.reshape(n_tok, d)
    x_bf = x.astype(BF16)
    o_f = 6 * width
    o_g = o_f + heads
    qkv = _matmul(x_bf, w_in[:, :o_f].astype(BF16), b_in[None, :o_f], out_dtype=BF16, act=None,
                  tm=cfg.mm_tm, tn=cfg.mm_tn, name="in_proj_qkv")
    gates = _matmul(x_bf, w_in[:, o_g:].astype(BF16), b_in[None, o_g:], out_dtype=BF16,
                    act="sigmoid", tm=cfg.mm_tm, tn=cfg.mm_tn, name="in_proj_gates")
    w_f = jnp.zeros((d, LANES), BF16).at[:, :heads].set(w_in[:, o_f:o_g].astype(BF16))
    b_f = jnp.zeros((1, LANES), F32).at[0, :heads].set(b_in[o_f:o_g])
    f_logit = _matmul(x_bf, w_f, b_f, out_dtype=F32, act=None,
                      tm=cfg.mm_tm, tn=LANES, name="in_proj_forget")
    f_rows = f_logit[:, :heads].reshape(b, s, heads).transpose(0, 2, 1).reshape(b * heads, s)
    c_rows = _forget_cumsum(f_rows).reshape(b, heads, 1, s)

    qkv3 = qkv.reshape(b, s, o_f)
    o_sb = _attention(qkv3, None, cfg, "sb").reshape(n_tok, width)
    o_fx = _attention(qkv3, c_rows, cfg, "fx").reshape(n_tok, width)

    h1 = _post_attention(o_sb, o_fx, gates, w_sb.astype(BF16), w_fx.astype(BF16),
                         w_out.astype(BF16), x, ln1_g[None, :], ln1_b[None, :], cfg, alpha)
    out = _moe(h1, w_router, b_router, w_up, b_up, w_down, b_down,
               ln2_g[None, :], ln2_b[None, :], cfg, alpha)
    return out.reshape(b, s, d)


def kernel(x, w_in, b_in, w_branch_sb, w_branch_fx, w_out, ln1_g, ln1_b, w_router, b_router,
           w_up, b_up, w_down, b_down, ln2_g, ln2_b):
    depth = w_in.shape[0]
    d = x.shape[-1]
    n_experts = w_router.shape[-1]
    heads = (w_in.shape[-1] - 2 * d) // (6 * HEAD_DIM + 1)
    cfg = Config(heads=heads, n_experts=n_experts, depth=depth,
                 mm_tm=1024, mm_tn=1024, attn_tile=256, attn_group=4, post_tm=512, post_tc=512,
                 route_tm=512, moe_tm=512, moe_tf=512, moe_tn=2048, comb_tt=128)
    h = x
    for l in range(depth):
        h = _layer(h, w_in[l], b_in[l], w_branch_sb[l], w_branch_fx[l], w_out[l],
                   ln1_g[l], ln1_b[l], w_router[l], b_router[l], w_up[l], b_up[l],
                   w_down[l], b_down[l], ln2_g[l], ln2_b[l], cfg)
    return h
```
